```python
import jax, jax.numpy as jnp
from jax import lax
import numpy as np

D_MODEL = 1024
BATCH = 4
SEQ = 4096
DEPTH = 4

CONV_WIDTH = D_MODEL
CONV_K = 3
N_Q_HEADS = 16
N_KV_HEADS = 4
HEAD_DIM = 64
ATTN_WIDTH = N_Q_HEADS * HEAD_DIM
KV_WIDTH = N_KV_HEADS * HEAD_DIM
WINDOW = 128
BLOCK = 128
N_BRANCHES = 2
EPS = 1e-6
NEG_INF = -1e30

IN_SIZES = (CONV_WIDTH, CONV_WIDTH, CONV_WIDTH, CONV_WIDTH,
            ATTN_WIDTH, KV_WIDTH, KV_WIDTH, ATTN_WIDTH,
            N_BRANCHES * D_MODEL)
IN_COLS = sum(IN_SIZES)
SPLIT_POINTS = tuple(int(c) for c in np.cumsum(IN_SIZES)[:-1])

kernel_name = "hybrid_shortconv_swa_sink_gated_block"


def rms_norm(x, g):
    xf = x.astype(jnp.float32)
    y = xf * lax.rsqrt(jnp.mean(xf * xf, axis=-1, keepdims=True) + EPS)
    return (y * g.astype(jnp.float32)).astype(x.dtype)


def causal_depthwise_conv(u, w):
    s = u.shape[1]
    up = jnp.pad(u, ((0, 0), (CONV_K - 1, 0), (0, 0)))
    y = up[:, 0:s] * w[0]
    for k in range(1, CONV_K):
        y = y + up[:, k:k + s] * w[k]
    return y


def sliding_window_attention(q, k, v, sinks):
    b, s = q.shape[0], q.shape[1]
    nb = s // BLOCK
    g = N_Q_HEADS // N_KV_HEADS
    qb = q.reshape(b, nb, BLOCK, N_KV_HEADS, g, HEAD_DIM)

    def band(t):
        tb = t.reshape(b, nb, BLOCK, N_KV_HEADS, HEAD_DIM)
        prev = jnp.pad(tb[:, :-1], ((0, 0), (1, 0), (0, 0), (0, 0), (0, 0)))
        return jnp.concatenate([prev, tb], axis=2)

    kb, vb = band(k), band(v)
    scale = HEAD_DIM ** -0.5
    scores = jnp.einsum('bnqhgd,bnkhd->bnhgqk', qb.astype(jnp.float32),
                        kb.astype(jnp.float32)) * scale
    blk = jnp.arange(nb)[:, None, None]
    q_pos = blk * BLOCK + jnp.arange(BLOCK)[None, :, None]
    k_pos = (blk - 1) * BLOCK + jnp.arange(2 * BLOCK)[None, None, :]
    diff = q_pos - k_pos
    valid = (diff >= 0) & (diff < WINDOW) & (k_pos >= 0)
    scores = jnp.where(valid[None, :, None, None], scores, NEG_INF)
    sink = sinks.astype(jnp.float32).reshape(N_KV_HEADS, g)[None, None, :, :, None, None]
    m = jnp.maximum(jnp.max(scores, axis=-1, keepdims=True), sink)
    p = jnp.exp(scores - m)
    p = p / (jnp.sum(p, axis=-1, keepdims=True) + jnp.exp(sink - m))
    out = jnp.einsum('bnhgqk,bnkhd->bnqhgd', p.astype(v.dtype), vb)
    return out.reshape(b, s, ATTN_WIDTH)


def hybrid_layer(x, norm_g, w_in, conv_w, q_norm_g, k_norm_g, sinks,
                 w_conv_out, w_attn_out, gate_b, w_out):
    b, s, _ = x.shape
    h = rms_norm(x, norm_g)
    u = jnp.einsum('bsd,dc->bsc', h, w_in)
    v_c, b_c, c_c, z_c, q, k, v, z_a, gate_logits = jnp.split(u, SPLIT_POINTS, axis=-1)

    y_c = b_c * causal_depthwise_conv(c_c * v_c, conv_w)
    y_c = y_c * jax.nn.silu(z_c)
    y_a = jnp.einsum('bsc,cd->bsd', y_c, w_conv_out)

    q = rms_norm(q.reshape(b, s, N_Q_HEADS, HEAD_DIM), q_norm_g)
    k = rms_norm(k.reshape(b, s, N_KV_HEADS, HEAD_DIM), k_norm_g)
    v = v.reshape(b, s, N_KV_HEADS, HEAD_DIM)
    o = sliding_window_attention(q, k, v, sinks) * jax.nn.silu(z_a)
    y_b = jnp.einsum('bsc,cd->bsd', o, w_attn_out)

    gates = jax.nn.sigmoid(gate_logits + gate_b)
    g_a, g_b = jnp.split(gates, N_BRANCHES, axis=-1)
    merged = g_a * y_a + g_b * y_b
    return x + jnp.einsum('bsd,de->bse', merged, w_out)


def setup_inputs(seed: int = 0) -> dict:
    key = jax.random.key(seed)
    ks = jax.random.split(key, 12)
    f32 = jnp.float32
    x = jax.random.normal(ks[0], (BATCH, SEQ, D_MODEL), f32)
    norm_g = 1.0 + 0.05 * jax.random.normal(ks[1], (DEPTH, D_MODEL), f32)
    w_in = jax.random.normal(ks[2], (DEPTH, D_MODEL, IN_COLS), f32) * D_MODEL ** -0.5
    conv_w = jax.random.normal(ks[3], (DEPTH, CONV_K, CONV_WIDTH), f32) * CONV_K ** -0.5
    q_norm_g = 1.0 + 0.05 * jax.random.normal(ks[4], (DEPTH, HEAD_DIM), f32)
    k_norm_g = 1.0 + 0.05 * jax.random.normal(ks[5], (DEPTH, HEAD_DIM), f32)
    sinks = 0.5 * jax.random.normal(ks[6], (DEPTH, N_Q_HEADS), f32)
    w_conv_out = jax.random.normal(ks[7], (DEPTH, CONV_WIDTH, D_MODEL), f32) * CONV_WIDTH ** -0.5
    w_attn_out = jax.random.normal(ks[8], (DEPTH, ATTN_WIDTH, D_MODEL), f32) * ATTN_WIDTH ** -0.5
    gate_b = 0.02 * jax.random.normal(ks[9], (DEPTH, N_BRANCHES * D_MODEL), f32)
    w_out = jax.random.normal(ks[10], (DEPTH, D_MODEL, D_MODEL), f32) * D_MODEL ** -0.5
    return {"x": x, "norm_g": norm_g, "w_in": w_in, "conv_w": conv_w,
            "q_norm_g": q_norm_g, "k_norm_g": k_norm_g, "sinks": sinks,
            "w_conv_out": w_conv_out, "w_attn_out": w_attn_out,
            "gate_b": gate_b, "w_out": w_out}


def reference(x, norm_g, w_in, conv_w, q_norm_g, k_norm_g, sinks,
              w_conv_out, w_attn_out, gate_b, w_out):
    for l in range(DEPTH):
        x = hybrid_layer(x, norm_g[l], w_in[l], conv_w[l], q_norm_g[l], k_norm_g[l],
                         sinks[l], w_conv_out[l], w_attn_out[l], gate_b[l], w_out[l])
    return x
```

```python
import functools

import jax
import jax.numpy as jnp
from jax import lax
from jax.experimental import pallas as pl
from jax.experimental.pallas import tpu as pltpu

D_MODEL = 1024
CONV_WIDTH = D_MODEL
CONV_K = 3
N_Q_HEADS = 16
N_KV_HEADS = 4
HEAD_DIM = 64
ATTN_WIDTH = N_Q_HEADS * HEAD_DIM
KV_WIDTH = N_KV_HEADS * HEAD_DIM
WINDOW = 128
BLOCK = 128
N_BRANCHES = 2
EPS = 1e-6
NEG_INF = -1e30

OFF_VC = 0
OFF_BC = OFF_VC + CONV_WIDTH
OFF_CC = OFF_BC + CONV_WIDTH
OFF_ZC = OFF_CC + CONV_WIDTH
OFF_Q = OFF_ZC + CONV_WIDTH
OFF_K = OFF_Q + ATTN_WIDTH
OFF_V = OFF_K + KV_WIDTH
OFF_ZA = OFF_V + KV_WIDTH
OFF_GATE = OFF_ZA + ATTN_WIDTH
IN_COLS = OFF_GATE + N_BRANCHES * D_MODEL

LANES = 128
SUBLANES = 8
TQ = 512
NC = 256
HEADS_PER_LANE_BLOCK = LANES // HEAD_DIM
GROUP = N_Q_HEADS // N_KV_HEADS
VMEM_LIMIT_BYTES = 58 * 1024 * 1024

_BF16 = jnp.bfloat16
_F32 = jnp.float32


def _dot(a, b):
    return jnp.dot(a, b, preferred_element_type=_F32)


def _dot_nt(a, b):
    return lax.dot_general(a, b, (((1,), (1,)), ((), ())), preferred_element_type=_F32)


def _head_rms_norm(x, gain, low_half):
    sq = x * x
    ss_lo = jnp.sum(jnp.where(low_half, sq, 0.0), axis=-1, keepdims=True)
    ss_hi = jnp.sum(jnp.where(low_half, 0.0, sq), axis=-1, keepdims=True)
    r_lo = lax.rsqrt(ss_lo * (1.0 / HEAD_DIM) + EPS)
    r_hi = lax.rsqrt(ss_hi * (1.0 / HEAD_DIM) + EPS)
    return (x * jnp.where(low_half, r_lo, r_hi)) * gain


def _layer_kernel(sinks_ref, x_ref, ng_ref, win_ref, convw_ref, qg_ref, kg_ref,
                  wco_ref, wao_ref, gb_ref, wout_ref, out_ref,
                  h_ref, cvbuf_ref, cvcarry_ref, yc_ref, qlo_ref, qhi_ref,
                  kk_ref, vlo_ref, vhi_ref, za_ref, o_ref, mg_ref):
    t = pl.program_id(1)
    tq = x_ref.shape[1]
    n_blocks = tq // BLOCK

    @pl.when(t == 0)
    def _():
        cvcarry_ref[...] = jnp.zeros_like(cvcarry_ref)
        kk_ref[:, 0:BLOCK, :] = jnp.zeros((N_KV_HEADS, BLOCK, LANES), _BF16)
        vlo_ref[:, 0:BLOCK, :] = jnp.zeros((N_KV_HEADS, BLOCK, LANES), _BF16)
        vhi_ref[:, 0:BLOCK, :] = jnp.zeros((N_KV_HEADS, BLOCK, LANES), _BF16)

    x = x_ref[0]
    ms = jnp.mean(x * x, axis=-1, keepdims=True)
    h_ref[...] = ((x * lax.rsqrt(ms + EPS)) * ng_ref[...]).astype(_BF16)

    for c in range(CONV_WIDTH // NC):
        cols = slice(c * NC, (c + 1) * NC)
        v_c = _dot(h_ref[...], win_ref[:, OFF_VC + c * NC:OFF_VC + (c + 1) * NC])
        c_c = _dot(h_ref[...], win_ref[:, OFF_CC + c * NC:OFF_CC + (c + 1) * NC])
        cv = c_c * v_c
        cvbuf_ref[0:SUBLANES, :] = cvcarry_ref[:, cols]
        cvbuf_ref[SUBLANES:SUBLANES + tq, :] = cv
        cvcarry_ref[:, cols] = cv[tq - SUBLANES:tq, :]
        w = convw_ref[:, cols]
        y = cvbuf_ref[SUBLANES - 2:SUBLANES - 2 + tq, :] * w[0:1, :]
        y = y + cvbuf_ref[SUBLANES - 1:SUBLANES - 1 + tq, :] * w[1:2, :]
        y = y + cv * w[2:3, :]
        b_c = _dot(h_ref[...], win_ref[:, OFF_BC + c * NC:OFF_BC + (c + 1) * NC])
        z_c = _dot(h_ref[...], win_ref[:, OFF_ZC + c * NC:OFF_ZC + (c + 1) * NC])
        yc_ref[:, cols] = ((b_c * y) * jax.nn.silu(z_c)).astype(_BF16)

    lane = lax.broadcasted_iota(jnp.int32, (tq, LANES), 1)
    low_half = lane < HEAD_DIM
    scale = HEAD_DIM ** -0.5
    for n in range(ATTN_WIDTH // NC):
        q = _dot(h_ref[...], win_ref[:, OFF_Q + n * NC:OFF_Q + (n + 1) * NC])
        for half in range(NC // LANES):
            lb = slice((n * (NC // LANES) + half) * LANES, (n * (NC // LANES) + half + 1) * LANES)
            qs = _head_rms_norm(q[:, half * LANES:(half + 1) * LANES], qg_ref[...], low_half) * scale
            qlo_ref[:, lb] = jnp.where(low_half, qs, 0.0).astype(_BF16)
            qhi_ref[:, lb] = jnp.where(low_half, 0.0, qs).astype(_BF16)
        z_a = _dot(h_ref[...], win_ref[:, OFF_ZA + n * NC:OFF_ZA + (n + 1) * NC])
        za_ref[:, n * NC:(n + 1) * NC] = jax.nn.silu(z_a)

    k = _dot(h_ref[...], win_ref[:, OFF_K:OFF_K + KV_WIDTH])
    v = _dot(h_ref[...], win_ref[:, OFF_V:OFF_V + KV_WIDTH])
    cur = slice(BLOCK, BLOCK + tq)
    for mk in range(KV_WIDTH // LANES):
        kp = _head_rms_norm(k[:, mk * LANES:(mk + 1) * LANES], kg_ref[...], low_half)
        kr = pltpu.roll(kp, HEAD_DIM, axis=1)
        vp = v[:, mk * LANES:(mk + 1) * LANES]
        vr = pltpu.roll(vp, HEAD_DIM, axis=1)
        g_even = mk * HEADS_PER_LANE_BLOCK
        g_odd = g_even + 1
        kk_ref[g_even, cur, :] = jnp.where(low_half, kp, kr).astype(_BF16)
        kk_ref[g_odd, cur, :] = jnp.where(low_half, kr, kp).astype(_BF16)
        vlo_ref[g_even, cur, :] = jnp.where(low_half, vp, 0.0).astype(_BF16)
        vhi_ref[g_even, cur, :] = jnp.where(low_half, 0.0, vr).astype(_BF16)
        vlo_ref[g_odd, cur, :] = jnp.where(low_half, vr, 0.0).astype(_BF16)
        vhi_ref[g_odd, cur, :] = jnp.where(low_half, 0.0, vp).astype(_BF16)

    qi = lax.broadcasted_iota(jnp.int32, (BLOCK, 2 * BLOCK), 0)
    kc = lax.broadcasted_iota(jnp.int32, (BLOCK, 2 * BLOCK), 1)
    dist = kc - qi
    band = (dist > 0) & (dist <= WINDOW)
    band_first = band & ((kc >= BLOCK) | (t > 0))
    low_half_b = lax.broadcasted_iota(jnp.int32, (BLOCK, LANES), 1) < HEAD_DIM

    def softmax_unnormalised(s, valid, sink):
        s = jnp.where(valid, s, NEG_INF)
        m = jnp.maximum(jnp.max(s, axis=-1, keepdims=True), sink)
        p = jnp.exp(s - m)
        den = jnp.sum(p, axis=-1, keepdims=True) + jnp.exp(sink - m)
        return p.astype(_BF16), den

    for b in range(n_blocks):
        rows = slice(b * BLOCK, (b + 1) * BLOCK)
        win = slice(b * BLOCK, b * BLOCK + 2 * BLOCK)
        valid = band_first if b == 0 else band
        for m in range(ATTN_WIDTH // LANES):
            lb = slice(m * LANES, (m + 1) * LANES)
            g = (m * HEADS_PER_LANE_BLOCK) // GROUP
            kwin = kk_ref[g, win, :]
            s0 = _dot_nt(qlo_ref[rows, lb], kwin)
            s1 = _dot_nt(qhi_ref[rows, lb], kwin)
            p0, d0 = softmax_unnormalised(s0, valid, sinks_ref[m * HEADS_PER_LANE_BLOCK])
            p1, d1 = softmax_unnormalised(s1, valid, sinks_ref[m * HEADS_PER_LANE_BLOCK + 1])
            o = _dot(p0, vlo_ref[g, win, :]) + _dot(p1, vhi_ref[g, win, :])
            o = o / jnp.where(low_half_b, d0, d1)
            o_ref[rows, lb] = (o * za_ref[rows, lb]).astype(_BF16)

    last = slice(tq, tq + BLOCK)
    kk_ref[:, 0:BLOCK, :] = kk_ref[:, last, :]
    vlo_ref[:, 0:BLOCK, :] = vlo_ref[:, last, :]
    vhi_ref[:, 0:BLOCK, :] = vhi_ref[:, last, :]

    for n in range(D_MODEL // NC):
        cols = slice(n * NC, (n + 1) * NC)
        y_a = _dot(yc_ref[...], wco_ref[:, cols])
        y_b = _dot(o_ref[...], wao_ref[:, cols])
        g_a = jax.nn.sigmoid(
            _dot(h_ref[...], win_ref[:, OFF_GATE + n * NC:OFF_GATE + (n + 1) * NC]) + gb_ref[:, cols])
        g_b = jax.nn.sigmoid(
            _dot(h_ref[...], win_ref[:, OFF_GATE + D_MODEL + n * NC:OFF_GATE + D_MODEL + (n + 1) * NC])
            + gb_ref[:, D_MODEL + n * NC:D_MODEL + (n + 1) * NC])
        mg_ref[:, cols] = (g_a * y_a + g_b * y_b).astype(_BF16)
    for n in range(D_MODEL // NC):
        cols = slice(n * NC, (n + 1) * NC)
        out_ref[0, :, cols] = x_ref[0, :, cols] + _dot(mg_ref[...], wout_ref[:, cols])


def _resident(shape):
    return pl.BlockSpec(shape, lambda b, t: (0,) * len(shape), pipeline_mode=pl.Buffered(1))


@functools.partial(jax.jit, static_argnames=("tq",))
def _hybrid_layer(x, norm_g, w_in, conv_w, q_norm_g, k_norm_g, sinks,
                  w_conv_out, w_attn_out, gate_b, w_out, *, tq):
    batch, seq, d = x.shape
    assert d == D_MODEL and seq % tq == 0 and tq % BLOCK == 0
    assert w_in.shape == (D_MODEL, IN_COLS)
    grid = (batch, seq // tq)
    kv_rows = tq + BLOCK
    return pl.pallas_call(
        _layer_kernel,
        grid=grid,
        in_specs=[
            pl.BlockSpec(memory_space=pltpu.SMEM),
            pl.BlockSpec((1, tq, D_MODEL), lambda b, t: (b, t, 0)),
            _resident((1, D_MODEL)),
            _resident((D_MODEL, IN_COLS)),
            _resident((CONV_K, CONV_WIDTH)),
            _resident((1, LANES)),
            _resident((1, LANES)),
            _resident((CONV_WIDTH, D_MODEL)),
            _resident((ATTN_WIDTH, D_MODEL)),
            _resident((1, N_BRANCHES * D_MODEL)),
            _resident((D_MODEL, D_MODEL)),
        ],
        out_specs=pl.BlockSpec((1, tq, D_MODEL), lambda b, t: (b, t, 0)),
        out_shape=jax.ShapeDtypeStruct(x.shape, x.dtype),
        scratch_shapes=[
            pltpu.VMEM((tq, D_MODEL), _BF16),
            pltpu.VMEM((tq + SUBLANES, NC), _F32),
            pltpu.VMEM((SUBLANES, CONV_WIDTH), _F32),
            pltpu.VMEM((tq, CONV_WIDTH), _BF16),
            pltpu.VMEM((tq, ATTN_WIDTH), _BF16),
            pltpu.VMEM((tq, ATTN_WIDTH), _BF16),
            pltpu.VMEM((N_KV_HEADS, kv_rows, LANES), _BF16),
            pltpu.VMEM((N_KV_HEADS, kv_rows, LANES), _BF16),
            pltpu.VMEM((N_KV_HEADS, kv_rows, LANES), _BF16),
            pltpu.VMEM((tq, ATTN_WIDTH), _F32),
            pltpu.VMEM((tq, ATTN_WIDTH), _BF16),
            pltpu.VMEM((tq, D_MODEL), _BF16),
        ],
        compiler_params=pltpu.CompilerParams(
            dimension_semantics=("arbitrary", "arbitrary"),
            vmem_limit_bytes=VMEM_LIMIT_BYTES,
        ),
        name="hybrid_layer",
    )(sinks, x, norm_g, w_in, conv_w, q_norm_g, k_norm_g, w_conv_out, w_attn_out, gate_b, w_out)


def kernel(x, norm_g, w_in, conv_w, q_norm_g, k_norm_g, sinks, w_conv_out, w_attn_out, gate_b, w_out):
    depth = norm_g.shape[0]
    tq = min(TQ, x.shape[1])
    w_in_b = w_in.astype(_BF16)
    w_co_b = w_conv_out.astype(_BF16)
    w_ao_b = w_attn_out.astype(_BF16)
    w_out_b = w_out.astype(_BF16)
    qg2 = jnp.tile(q_norm_g, (1, HEADS_PER_LANE_BLOCK))[:, None, :]
    kg2 = jnp.tile(k_norm_g, (1, HEADS_PER_LANE_BLOCK))[:, None, :]
    for l in range(depth):
        x = _hybrid_layer(x, norm_g[l][None, :], w_in_b[l], conv_w[l], qg2[l], kg2[l], sinks[l],
                          w_co_b[l], w_ao_b[l], gate_b[l][None, :], w_out_b[l], tq=tq)
    return x
```

```python
import functools

import jax
import jax.numpy as jnp
from jax import lax
from jax.experimental import pallas as pl
from jax.experimental.pallas import tpu as pltpu

D_MODEL = 1024
CONV_WIDTH = D_MODEL
CONV_K = 3
N_Q_HEADS = 16
N_KV_HEADS = 4
HEAD_DIM = 64
ATTN_WIDTH = N_Q_HEADS * HEAD_DIM
KV_WIDTH = N_KV_HEADS * HEAD_DIM
WINDOW = 128
BLOCK = 128
N_BRANCHES = 2
EPS = 1e-6
NEG_INF = -1e30

OFF_VC = 0
OFF_BC = OFF_VC + CONV_WIDTH
OFF_CC = OFF_BC + CONV_WIDTH
OFF_ZC = OFF_CC + CONV_WIDTH
OFF_Q = OFF_ZC + CONV_WIDTH
OFF_K = OFF_Q + ATTN_WIDTH
OFF_V = OFF_K + KV_WIDTH
OFF_ZA = OFF_V + KV_WIDTH
OFF_GATE = OFF_ZA + ATTN_WIDTH
IN_COLS = OFF_GATE + N_BRANCHES * D_MODEL

LANES = 128
SUBLANES = 8
TQ = 512
NC = 256
HEADS_PER_LANE_BLOCK = LANES // HEAD_DIM
GROUP = N_Q_HEADS // N_KV_HEADS
LANE_BLOCKS_PER_GROUP = GROUP // HEADS_PER_LANE_BLOCK
ATTN_LAG = 2
VMEM_LIMIT_BYTES = 58 * 1024 * 1024

SLOT_TO_HEAD_IN_GROUP = tuple(
    lb * HEADS_PER_LANE_BLOCK + half
    for half in range(HEADS_PER_LANE_BLOCK) for lb in range(LANE_BLOCKS_PER_GROUP))

_BF16 = jnp.bfloat16
_F32 = jnp.float32


def _dot(a, b):
    return jnp.dot(a, b, preferred_element_type=_F32)


def _dot_scores(q, k):
    return lax.dot_general(q, k, (((1,), (1,)), ((), ())), preferred_element_type=_F32)


def _dot_pv(p, v):
    return jnp.dot(p, v, preferred_element_type=_F32)


def _head_rms_norm(x, gain, low_half):
    sq = x * x
    ss_lo = jnp.sum(jnp.where(low_half, sq, 0.0), axis=-1, keepdims=True)
    ss_hi = jnp.sum(jnp.where(low_half, 0.0, sq), axis=-1, keepdims=True)
    r_lo = lax.rsqrt(ss_lo * (1.0 / HEAD_DIM) + EPS)
    r_hi = lax.rsqrt(ss_hi * (1.0 / HEAD_DIM) + EPS)
    return (x * jnp.where(low_half, r_lo, r_hi)) * gain


def _layer_kernel(sinks_ref, x_ref, ng_ref, win_ref, convw_ref, qg_ref, kg_ref,
                  wco_ref, wao_ref, gb_ref, wout_ref, out_ref,
                  h_ref, cvbuf_ref, cvcarry_ref, yc_ref, qm_ref,
                  kk_ref, vlo_ref, vhi_ref, za_ref, o_ref, mg_ref):
    t = pl.program_id(1)
    tq = x_ref.shape[1]
    n_blocks = tq // BLOCK

    @pl.when(t == 0)
    def _():
        cvcarry_ref[...] = jnp.zeros_like(cvcarry_ref)
        kk_ref[:, 0:BLOCK, :] = jnp.zeros((N_KV_HEADS, BLOCK, LANES), _BF16)
        vlo_ref[:, 0:BLOCK, :] = jnp.zeros((N_KV_HEADS, BLOCK, LANES), _BF16)
        vhi_ref[:, 0:BLOCK, :] = jnp.zeros((N_KV_HEADS, BLOCK, LANES), _BF16)

    x = x_ref[0]
    ms = jnp.mean(x * x, axis=-1, keepdims=True)
    h_ref[...] = ((x * lax.rsqrt(ms + EPS)) * ng_ref[...]).astype(_BF16)

    lane = lax.broadcasted_iota(jnp.int32, (tq, LANES), 1)
    low_half = lane < HEAD_DIM
    scale = HEAD_DIM ** -0.5
    for n in range(ATTN_WIDTH // NC):
        q = _dot(h_ref[...], win_ref[:, OFF_Q + n * NC:OFF_Q + (n + 1) * NC])
        for half in range(NC // LANES):
            m = n * (NC // LANES) + half
            g, lb_in_group = divmod(m, LANE_BLOCKS_PER_GROUP)
            qs = _head_rms_norm(q[:, half * LANES:(half + 1) * LANES], qg_ref[...], low_half) * scale
            qm_ref[g * GROUP + lb_in_group] = jnp.where(low_half, qs, 0.0).astype(_BF16)
            qm_ref[g * GROUP + LANE_BLOCKS_PER_GROUP + lb_in_group] = (
                jnp.where(low_half, 0.0, qs).astype(_BF16))
        z_a = _dot(h_ref[...], win_ref[:, OFF_ZA + n * NC:OFF_ZA + (n + 1) * NC])
        za_ref[:, n * NC:(n + 1) * NC] = jax.nn.silu(z_a)

    k = _dot(h_ref[...], win_ref[:, OFF_K:OFF_K + KV_WIDTH])
    v = _dot(h_ref[...], win_ref[:, OFF_V:OFF_V + KV_WIDTH])
    cur = slice(BLOCK, BLOCK + tq)
    for mk in range(KV_WIDTH // LANES):
        kp = _head_rms_norm(k[:, mk * LANES:(mk + 1) * LANES], kg_ref[...], low_half)
        kr = pltpu.roll(kp, HEAD_DIM, axis=1)
        vp = v[:, mk * LANES:(mk + 1) * LANES]
        vr = pltpu.roll(vp, HEAD_DIM, axis=1)
        g_even = mk * HEADS_PER_LANE_BLOCK
        g_odd = g_even + 1
        kk_ref[g_even, cur, :] = jnp.where(low_half, kp, kr).astype(_BF16)
        kk_ref[g_odd, cur, :] = jnp.where(low_half, kr, kp).astype(_BF16)
        vlo_ref[g_even, cur, :] = jnp.where(low_half, vp, 0.0).astype(_BF16)
        vhi_ref[g_even, cur, :] = jnp.where(low_half, 0.0, vr).astype(_BF16)
        vlo_ref[g_odd, cur, :] = jnp.where(low_half, vr, 0.0).astype(_BF16)
        vhi_ref[g_odd, cur, :] = jnp.where(low_half, 0.0, vp).astype(_BF16)

    def conv_stages():
        for c in range(CONV_WIDTH // NC):
            cols = slice(c * NC, (c + 1) * NC)
            v_c = _dot(h_ref[...], win_ref[:, OFF_VC + c * NC:OFF_VC + (c + 1) * NC])
            yield
            c_c = _dot(h_ref[...], win_ref[:, OFF_CC + c * NC:OFF_CC + (c + 1) * NC])
            cv = c_c * v_c
            cvbuf_ref[0:SUBLANES, :] = cvcarry_ref[:, cols]
            cvbuf_ref[SUBLANES:SUBLANES + tq, :] = cv
            cvcarry_ref[:, cols] = cv[tq - SUBLANES:tq, :]
            w = convw_ref[:, cols]
            y = cvbuf_ref[SUBLANES - 2:SUBLANES - 2 + tq, :] * w[0:1, :]
            y = y + cvbuf_ref[SUBLANES - 1:SUBLANES - 1 + tq, :] * w[1:2, :]
            y = y + cv * w[2:3, :]
            yield
            by = _dot(h_ref[...], win_ref[:, OFF_BC + c * NC:OFF_BC + (c + 1) * NC]) * y
            yield
            z_c = _dot(h_ref[...], win_ref[:, OFF_ZC + c * NC:OFF_ZC + (c + 1) * NC])
            yc_ref[:, cols] = (by * jax.nn.silu(z_c)).astype(_BF16)
            yield

    qi = lax.broadcasted_iota(jnp.int32, (BLOCK, 2 * BLOCK), 0)
    kc = lax.broadcasted_iota(jnp.int32, (BLOCK, 2 * BLOCK), 1)
    dist = kc - qi
    band = (dist > 0) & (dist <= WINDOW)
    band_first = band & ((kc >= BLOCK) | (t > 0))
    low_half_pair = lax.broadcasted_iota(
        jnp.int32, (LANE_BLOCKS_PER_GROUP * BLOCK, LANES), 1) < HEAD_DIM

    def softmax_unnormalised(s, valid, sink):
        s = jnp.where(valid, s, NEG_INF)
        m = jnp.maximum(jnp.max(s, axis=-1, keepdims=True), sink)
        p = jnp.exp(s - m)
        den = jnp.sum(p, axis=-1, keepdims=True) + jnp.exp(sink - m)
        return p.astype(_BF16), den

    conv = conv_stages()

    def scores(b, g):
        rows = slice(b * BLOCK, (b + 1) * BLOCK)
        win = slice(b * BLOCK, b * BLOCK + 2 * BLOCK)
        q_stack = qm_ref[g * GROUP:(g + 1) * GROUP, rows, :].reshape(GROUP * BLOCK, LANES)
        return _dot_scores(q_stack, kk_ref[g, win, :])

    def attend(b, g, s):
        rows = slice(b * BLOCK, (b + 1) * BLOCK)
        win = slice(b * BLOCK, b * BLOCK + 2 * BLOCK)
        valid = band_first if b == 0 else band
        ps, dens = [], []
        for slot in range(GROUP):
            head = g * GROUP + SLOT_TO_HEAD_IN_GROUP[slot]
            p, den = softmax_unnormalised(s[slot * BLOCK:(slot + 1) * BLOCK], valid, sinks_ref[head])
            ps.append(p)
            dens.append(den)
        half_rows = LANE_BLOCKS_PER_GROUP
        p_low = jnp.concatenate(ps[:half_rows], axis=0)
        p_high = jnp.concatenate(ps[half_rows:], axis=0)
        o = _dot_pv(p_low, vlo_ref[g, win, :]) + _dot_pv(p_high, vhi_ref[g, win, :])
        den = jnp.where(low_half_pair,
                        jnp.concatenate(dens[:half_rows], axis=0),
                        jnp.concatenate(dens[half_rows:], axis=0))
        o = o / den
        for i in range(LANE_BLOCKS_PER_GROUP):
            lb = slice((g * LANE_BLOCKS_PER_GROUP + i) * LANES, (g * LANE_BLOCKS_PER_GROUP + i + 1) * LANES)
            o_ref[rows, lb] = (o[i * BLOCK:(i + 1) * BLOCK] * za_ref[rows, lb]).astype(_BF16)

    steps = [(b, g) for b in range(n_blocks) for g in range(N_KV_HEADS)]
    in_flight = {}
    for i in range(len(steps) + ATTN_LAG):
        if i < len(steps):
            in_flight[i] = scores(*steps[i])
        next(conv, None)
        if i >= ATTN_LAG:
            attend(*steps[i - ATTN_LAG], in_flight.pop(i - ATTN_LAG))
    for _ in conv:
        pass

    last = slice(tq, tq + BLOCK)
    kk_ref[:, 0:BLOCK, :] = kk_ref[:, last, :]
    vlo_ref[:, 0:BLOCK, :] = vlo_ref[:, last, :]
    vhi_ref[:, 0:BLOCK, :] = vhi_ref[:, last, :]

    for n in range(D_MODEL // NC):
        cols = slice(n * NC, (n + 1) * NC)
        y_a = _dot(yc_ref[...], wco_ref[:, cols])
        y_b = _dot(o_ref[...], wao_ref[:, cols])
        g_a = jax.nn.sigmoid(
            _dot(h_ref[...], win_ref[:, OFF_GATE + n * NC:OFF_GATE + (n + 1) * NC]) + gb_ref[:, cols])
        g_b = jax.nn.sigmoid(
            _dot(h_ref[...], win_ref[:, OFF_GATE + D_MODEL + n * NC:OFF_GATE + D_MODEL + (n + 1) * NC])
            + gb_ref[:, D_MODEL + n * NC:D_MODEL + (n + 1) * NC])
        mg_ref[:, cols] = (g_a * y_a + g_b * y_b).astype(_BF16)
    for n in range(D_MODEL // NC):
        cols = slice(n * NC, (n + 1) * NC)
        out_ref[0, :, cols] = x_ref[0, :, cols] + _dot(mg_ref[...], wout_ref[:, cols])


def _resident(shape):
    return pl.BlockSpec(shape, lambda b, t: (0,) * len(shape), pipeline_mode=pl.Buffered(1))


@functools.partial(jax.jit, static_argnames=("tq",))
def _hybrid_layer(x, norm_g, w_in, conv_w, q_norm_g, k_norm_g, sinks,
                  w_conv_out, w_attn_out, gate_b, w_out, *, tq):
    batch, seq, d = x.shape
    assert d == D_MODEL and seq % tq == 0 and tq % BLOCK == 0
    assert w_in.shape == (D_MODEL, IN_COLS)
    grid = (batch, seq // tq)
    kv_rows = tq + BLOCK
    return pl.pallas_call(
        _layer_kernel,
        grid=grid,
        in_specs=[
            pl.BlockSpec(memory_space=pltpu.SMEM),
            pl.BlockSpec((1, tq, D_MODEL), lambda b, t: (b, t, 0)),
            _resident((1, D_MODEL)),
            _resident((D_MODEL, IN_COLS)),
            _resident((CONV_K, CONV_WIDTH)),
            _resident((1, LANES)),
            _resident((1, LANES)),
            _resident((CONV_WIDTH, D_MODEL)),
            _resident((ATTN_WIDTH, D_MODEL)),
            _resident((1, N_BRANCHES * D_MODEL)),
            _resident((D_MODEL, D_MODEL)),
        ],
        out_specs=pl.BlockSpec((1, tq, D_MODEL), lambda b, t: (b, t, 0)),
        out_shape=jax.ShapeDtypeStruct(x.shape, x.dtype),
        scratch_shapes=[
            pltpu.VMEM((tq, D_MODEL), _BF16),
            pltpu.VMEM((tq + SUBLANES, NC), _F32),
            pltpu.VMEM((SUBLANES, CONV_WIDTH), _F32),
            pltpu.VMEM((tq, CONV_WIDTH), _BF16),
            pltpu.VMEM((N_Q_HEADS, tq, LANES), _BF16),
            pltpu.VMEM((N_KV_HEADS, kv_rows, LANES), _BF16),
            pltpu.VMEM((N_KV_HEADS, kv_rows, LANES), _BF16),
            pltpu.VMEM((N_KV_HEADS, kv_rows, LANES), _BF16),
            pltpu.VMEM((tq, ATTN_WIDTH), _F32),
            pltpu.VMEM((tq, ATTN_WIDTH), _BF16),
            pltpu.VMEM((tq, D_MODEL), _BF16),
        ],
        compiler_params=pltpu.CompilerParams(
            dimension_semantics=("arbitrary", "arbitrary"),
            vmem_limit_bytes=VMEM_LIMIT_BYTES,
        ),
        name="hybrid_layer",
    )(sinks, x, norm_g, w_in, conv_w, q_norm_g, k_norm_g, w_conv_out, w_attn_out, gate_b, w_out)


def kernel(x, norm_g, w_in, conv_w, q_norm_g, k_norm_g, sinks, w_conv_out, w_attn_out, gate_b, w_out):
    depth = norm_g.shape[0]
    tq = min(TQ, x.shape[1])
    qg2 = jnp.tile(q_norm_g, (1, HEADS_PER_LANE_BLOCK))[:, None, :]
    kg2 = jnp.tile(k_norm_g, (1, HEADS_PER_LANE_BLOCK))[:, None, :]
    for l in range(depth):
        x = _hybrid_layer(x, norm_g[l][None, :], w_in[l].astype(_BF16), conv_w[l], qg2[l], kg2[l], sinks[l],
                          w_conv_out[l].astype(_BF16), w_attn_out[l].astype(_BF16), gate_b[l][None, :],
                          w_out[l].astype(_BF16), tq=tq)
    return x
```

```python
import functools

import jax
import jax.numpy as jnp
from jax import lax
from jax.experimental import pallas as pl
from jax.experimental.pallas import tpu as pltpu

D_MODEL = 1024
CONV_WIDTH = D_MODEL
CONV_K = 3
N_Q_HEADS = 16
N_KV_HEADS = 4
HEAD_DIM = 64
ATTN_WIDTH = N_Q_HEADS * HEAD_DIM
KV_WIDTH = N_KV_HEADS * HEAD_DIM
WINDOW = 128
BLOCK = 128
N_BRANCHES = 2
EPS = 1e-6
NEG_INF = -1e30

OFF_VC = 0
OFF_BC = OFF_VC + CONV_WIDTH
OFF_CC = OFF_BC + CONV_WIDTH
OFF_ZC = OFF_CC + CONV_WIDTH
OFF_Q = OFF_ZC + CONV_WIDTH
OFF_K = OFF_Q + ATTN_WIDTH
OFF_V = OFF_K + KV_WIDTH
OFF_ZA = OFF_V + KV_WIDTH
OFF_GATE = OFF_ZA + ATTN_WIDTH
IN_COLS = OFF_GATE + N_BRANCHES * D_MODEL

LANES = 128
SUBLANES = 8
MXU_WIDTH = 256
TQ = 512
NC = 2 * MXU_WIDTH
HEADS_PER_LANE_BLOCK = LANES // HEAD_DIM
GROUP = N_Q_HEADS // N_KV_HEADS
LANE_BLOCKS_PER_GROUP = GROUP // HEADS_PER_LANE_BLOCK
ATTN_LAG = 2
VMEM_LIMIT_BYTES = 58 * 1024 * 1024

SLOT_TO_HEAD_IN_GROUP = tuple(
    lb * HEADS_PER_LANE_BLOCK + half
    for half in range(HEADS_PER_LANE_BLOCK) for lb in range(LANE_BLOCKS_PER_GROUP))

_BF16 = jnp.bfloat16
_F32 = jnp.float32


def _dot(a, b):
    return jnp.dot(a, b, preferred_element_type=_F32)


def _dot_scores(q, k):
    return lax.dot_general(q, k, (((1,), (1,)), ((), ())), preferred_element_type=_F32)


def _dot_pv(p, v):
    return jnp.dot(p, v, preferred_element_type=_F32)


def _head_rms_norm(x, gain, low_half):
    sq = x * x
    ss_lo = jnp.sum(jnp.where(low_half, sq, 0.0), axis=-1, keepdims=True)
    ss_hi = jnp.sum(jnp.where(low_half, 0.0, sq), axis=-1, keepdims=True)
    r_lo = lax.rsqrt(ss_lo * (1.0 / HEAD_DIM) + EPS)
    r_hi = lax.rsqrt(ss_hi * (1.0 / HEAD_DIM) + EPS)
    return (x * jnp.where(low_half, r_lo, r_hi)) * gain


def _layer_kernel(sinks_ref, x_ref, ng_ref, win_ref, convw_ref, qg_ref, kg_ref,
                  wco_ref, wao_ref, gb_ref, wout_ref, out_ref,
                  h_ref, cvbuf_ref, cvcarry_ref, yc_ref, qm_ref,
                  kk_ref, vlo_ref, vhi_ref, za_ref, o_ref, gya_ref, gateb_ref, mg_ref):
    t = pl.program_id(1)
    tq = x_ref.shape[1]
    n_blocks = tq // BLOCK

    @pl.when(t == 0)
    def _():
        cvcarry_ref[...] = jnp.zeros_like(cvcarry_ref)
        kk_ref[:, 0:BLOCK, :] = jnp.zeros((N_KV_HEADS, BLOCK, LANES), _BF16)
        vlo_ref[:, 0:BLOCK, :] = jnp.zeros((N_KV_HEADS, BLOCK, LANES), _BF16)
        vhi_ref[:, 0:BLOCK, :] = jnp.zeros((N_KV_HEADS, BLOCK, LANES), _BF16)

    x = x_ref[0]
    ms = jnp.mean(x * x, axis=-1, keepdims=True)
    h_ref[...] = ((x * lax.rsqrt(ms + EPS)) * ng_ref[...]).astype(_BF16)

    def project(col, width=NC):
        return _dot(h_ref[...], win_ref[:, col:col + width])

    lane = lax.broadcasted_iota(jnp.int32, (tq, LANES), 1)
    low_half = lane < HEAD_DIM
    scale = HEAD_DIM ** -0.5
    for n in range(ATTN_WIDTH // NC):
        q = project(OFF_Q + n * NC)
        for j in range(NC // LANES):
            m = n * (NC // LANES) + j
            g, lb_in_group = divmod(m, LANE_BLOCKS_PER_GROUP)
            qs = _head_rms_norm(q[:, j * LANES:(j + 1) * LANES], qg_ref[...], low_half) * scale
            qm_ref[g * GROUP + lb_in_group] = jnp.where(low_half, qs, 0.0).astype(_BF16)
            qm_ref[g * GROUP + LANE_BLOCKS_PER_GROUP + lb_in_group] = (
                jnp.where(low_half, 0.0, qs).astype(_BF16))

    assert OFF_V == OFF_K + KV_WIDTH
    kv = project(OFF_K, 2 * KV_WIDTH)
    cur = slice(BLOCK, BLOCK + tq)
    for mk in range(KV_WIDTH // LANES):
        kp = _head_rms_norm(kv[:, mk * LANES:(mk + 1) * LANES], kg_ref[...], low_half)
        kr = pltpu.roll(kp, HEAD_DIM, axis=1)
        vp = kv[:, KV_WIDTH + mk * LANES:KV_WIDTH + (mk + 1) * LANES]
        vr = pltpu.roll(vp, HEAD_DIM, axis=1)
        g_even = mk * HEADS_PER_LANE_BLOCK
        g_odd = g_even + 1
        kk_ref[g_even, cur, :] = jnp.where(low_half, kp, kr).astype(_BF16)
        kk_ref[g_odd, cur, :] = jnp.where(low_half, kr, kp).astype(_BF16)
        vlo_ref[g_even, cur, :] = jnp.where(low_half, vp, 0.0).astype(_BF16)
        vhi_ref[g_even, cur, :] = jnp.where(low_half, 0.0, vr).astype(_BF16)
        vlo_ref[g_odd, cur, :] = jnp.where(low_half, vr, 0.0).astype(_BF16)
        vhi_ref[g_odd, cur, :] = jnp.where(low_half, 0.0, vp).astype(_BF16)

    def independent_stages():
        for n in range(ATTN_WIDTH // NC):
            za_ref[:, n * NC:(n + 1) * NC] = jax.nn.silu(project(OFF_ZA + n * NC))
            yield
        for c in range(CONV_WIDTH // NC):
            cols = slice(c * NC, (c + 1) * NC)
            v_c = project(OFF_VC + c * NC)
            yield
            cv = project(OFF_CC + c * NC) * v_c
            cvbuf_ref[0:SUBLANES, :] = cvcarry_ref[:, cols]
            cvbuf_ref[SUBLANES:SUBLANES + tq, :] = cv
            cvcarry_ref[:, cols] = cv[tq - SUBLANES:tq, :]
            w = convw_ref[:, cols]
            y = cvbuf_ref[SUBLANES - 2:SUBLANES - 2 + tq, :] * w[0:1, :]
            y = y + cvbuf_ref[SUBLANES - 1:SUBLANES - 1 + tq, :] * w[1:2, :]
            y = y + cv * w[2:3, :]
            yield
            by = project(OFF_BC + c * NC) * y
            yield
            yc_ref[:, cols] = (by * jax.nn.silu(project(OFF_ZC + c * NC))).astype(_BF16)
            yield
        for n in range(D_MODEL // NC):
            cols = slice(n * NC, (n + 1) * NC)
            gateb_ref[:, cols] = jax.nn.sigmoid(
                project(OFF_GATE + D_MODEL + n * NC) + gb_ref[:, D_MODEL + n * NC:D_MODEL + (n + 1) * NC])
            yield
        for n in range(D_MODEL // NC):
            cols = slice(n * NC, (n + 1) * NC)
            g_a = jax.nn.sigmoid(project(OFF_GATE + n * NC) + gb_ref[:, cols])
            yield
            gya_ref[:, cols] = g_a * _dot(yc_ref[...], wco_ref[:, cols])
            yield

    qi = lax.broadcasted_iota(jnp.int32, (BLOCK, 2 * BLOCK), 0)
    kc = lax.broadcasted_iota(jnp.int32, (BLOCK, 2 * BLOCK), 1)
    dist = kc - qi
    band = (dist > 0) & (dist <= WINDOW)
    band_first = band & ((kc >= BLOCK) | (t > 0))
    low_half_pair = lax.broadcasted_iota(
        jnp.int32, (LANE_BLOCKS_PER_GROUP * BLOCK, LANES), 1) < HEAD_DIM

    def softmax_unnormalised(s, valid, sink):
        s = jnp.where(valid, s, NEG_INF)
        m = jnp.maximum(jnp.max(s, axis=-1, keepdims=True), sink)
        p = jnp.exp(s - m)
        den = jnp.sum(p, axis=-1, keepdims=True) + jnp.exp(sink - m)
        return p.astype(_BF16), den

    def scores(b, g):
        rows = slice(b * BLOCK, (b + 1) * BLOCK)
        win = slice(b * BLOCK, b * BLOCK + 2 * BLOCK)
        q_stack = qm_ref[g * GROUP:(g + 1) * GROUP, rows, :].reshape(GROUP * BLOCK, LANES)
        return _dot_scores(q_stack, kk_ref[g, win, :])

    def attend(b, g, s):
        rows = slice(b * BLOCK, (b + 1) * BLOCK)
        win = slice(b * BLOCK, b * BLOCK + 2 * BLOCK)
        valid = band_first if b == 0 else band
        ps, dens = [], []
        for slot in range(GROUP):
            head = g * GROUP + SLOT_TO_HEAD_IN_GROUP[slot]
            p, den = softmax_unnormalised(s[slot * BLOCK:(slot + 1) * BLOCK], valid, sinks_ref[head])
            ps.append(p)
            dens.append(den)
        half_rows = LANE_BLOCKS_PER_GROUP
        p_low = jnp.concatenate(ps[:half_rows], axis=0)
        p_high = jnp.concatenate(ps[half_rows:], axis=0)
        o = _dot_pv(p_low, vlo_ref[g, win, :]) + _dot_pv(p_high, vhi_ref[g, win, :])
        den = jnp.where(low_half_pair,
                        jnp.concatenate(dens[:half_rows], axis=0),
                        jnp.concatenate(dens[half_rows:], axis=0))
        o = o / den
        for i in range(LANE_BLOCKS_PER_GROUP):
            lb = slice((g * LANE_BLOCKS_PER_GROUP + i) * LANES, (g * LANE_BLOCKS_PER_GROUP + i + 1) * LANES)
            o_ref[rows, lb] = (o[i * BLOCK:(i + 1) * BLOCK] * za_ref[rows, lb]).astype(_BF16)

    steps = [(b, g) for b in range(n_blocks) for g in range(N_KV_HEADS)]
    independent = independent_stages()
    in_flight = {}
    for i in range(len(steps) + ATTN_LAG):
        if i < len(steps):
            in_flight[i] = scores(*steps[i])
        next(independent, None)
        if i >= ATTN_LAG:
            attend(*steps[i - ATTN_LAG], in_flight.pop(i - ATTN_LAG))
    for _ in independent:
        pass

    last = slice(tq, tq + BLOCK)
    kk_ref[:, 0:BLOCK, :] = kk_ref[:, last, :]
    vlo_ref[:, 0:BLOCK, :] = vlo_ref[:, last, :]
    vhi_ref[:, 0:BLOCK, :] = vhi_ref[:, last, :]

    for n in range(D_MODEL // NC):
        cols = slice(n * NC, (n + 1) * NC)
        y_b = _dot(o_ref[...], wao_ref[:, cols])
        mg_ref[:, cols] = (gya_ref[:, cols] + gateb_ref[:, cols] * y_b).astype(_BF16)
    for n in range(D_MODEL // NC):
        cols = slice(n * NC, (n + 1) * NC)
        out_ref[0, :, cols] = x_ref[0, :, cols] + _dot(mg_ref[...], wout_ref[:, cols])


def _resident(shape):
    return pl.BlockSpec(shape, lambda b, t: (0,) * len(shape), pipeline_mode=pl.Buffered(1))


@functools.partial(jax.jit, static_argnames=("tq",))
def _hybrid_layer(x, norm_g, w_in, conv_w, q_norm_g, k_norm_g, sinks,
                  w_conv_out, w_attn_out, gate_b, w_out, *, tq):
    batch, seq, d = x.shape
    assert d == D_MODEL and seq % tq == 0 and tq % BLOCK == 0
    assert w_in.shape == (D_MODEL, IN_COLS)
    grid = (batch, seq // tq)
    kv_rows = tq + BLOCK
    return pl.pallas_call(
        _layer_kernel,
        grid=grid,
        in_specs=[
            pl.BlockSpec(memory_space=pltpu.SMEM),
            pl.BlockSpec((1, tq, D_MODEL), lambda b, t: (b, t, 0)),
            _resident((1, D_MODEL)),
            _resident((D_MODEL, IN_COLS)),
            _resident((CONV_K, CONV_WIDTH)),
            _resident((1, LANES)),
            _resident((1, LANES)),
            _resident((CONV_WIDTH, D_MODEL)),
            _resident((ATTN_WIDTH, D_MODEL)),
            _resident((1, N_BRANCHES * D_MODEL)),
            _resident((D_MODEL, D_MODEL)),
        ],
        out_specs=pl.BlockSpec((1, tq, D_MODEL), lambda b, t: (b, t, 0)),
        out_shape=jax.ShapeDtypeStruct(x.shape, x.dtype),
        scratch_shapes=[
            pltpu.VMEM((tq, D_MODEL), _BF16),
            pltpu.VMEM((tq + SUBLANES, NC), _F32),
            pltpu.VMEM((SUBLANES, CONV_WIDTH), _F32),
            pltpu.VMEM((tq, CONV_WIDTH), _BF16),
            pltpu.VMEM((N_Q_HEADS, tq, LANES), _BF16),
            pltpu.VMEM((N_KV_HEADS, kv_rows, LANES), _BF16),
            pltpu.VMEM((N_KV_HEADS, kv_rows, LANES), _BF16),
            pltpu.VMEM((N_KV_HEADS, kv_rows, LANES), _BF16),
            pltpu.VMEM((tq, ATTN_WIDTH), _F32),
            pltpu.VMEM((tq, ATTN_WIDTH), _BF16),
            pltpu.VMEM((tq, D_MODEL), _F32),
            pltpu.VMEM((tq, D_MODEL), _F32),
            pltpu.VMEM((tq, D_MODEL), _BF16),
        ],
        compiler_params=pltpu.CompilerParams(
            dimension_semantics=("arbitrary", "arbitrary"),
            vmem_limit_bytes=VMEM_LIMIT_BYTES,
        ),
        name="hybrid_layer",
    )(sinks, x, norm_g, w_in, conv_w, q_norm_g, k_norm_g, w_conv_out, w_attn_out, gate_b, w_out)


def kernel(x, norm_g, w_in, conv_w, q_norm_g, k_norm_g, sinks, w_conv_out, w_attn_out, gate_b, w_out):
    depth = norm_g.shape[0]
    tq = min(TQ, x.shape[1])
    qg2 = jnp.tile(q_norm_g, (1, HEADS_PER_LANE_BLOCK))[:, None, :]
    kg2 = jnp.tile(k_norm_g, (1, HEADS_PER_LANE_BLOCK))[:, None, :]
    for l in range(depth):
        x = _hybrid_layer(x, norm_g[l][None, :], w_in[l].astype(_BF16), conv_w[l], qg2[l], kg2[l], sinks[l],
                          w_conv_out[l].astype(_BF16), w_attn_out[l].astype(_BF16), gate_b[l][None, :],
                          w_out[l].astype(_BF16), tq=tq)
    return x
```

```python
import functools

import jax
import jax.numpy as jnp
from jax import lax
from jax.experimental import pallas as pl
from jax.experimental.pallas import tpu as pltpu

D_MODEL = 1024
CONV_WIDTH = D_MODEL
CONV_K = 3
N_Q_HEADS = 16
N_KV_HEADS = 4
HEAD_DIM = 64
ATTN_WIDTH = N_Q_HEADS * HEAD_DIM
KV_WIDTH = N_KV_HEADS * HEAD_DIM
WINDOW = 128
BLOCK = 128
N_BRANCHES = 2
EPS = 1e-6
NEG_INF = -1e30

OFF_VC = 0
OFF_BC = OFF_VC + CONV_WIDTH
OFF_CC = OFF_BC + CONV_WIDTH
OFF_ZC = OFF_CC + CONV_WIDTH
OFF_Q = OFF_ZC + CONV_WIDTH
OFF_K = OFF_Q + ATTN_WIDTH
OFF_V = OFF_K + KV_WIDTH
OFF_ZA = OFF_V + KV_WIDTH
OFF_GATE = OFF_ZA + ATTN_WIDTH
IN_COLS = OFF_GATE + N_BRANCHES * D_MODEL

LANES = 128
SUBLANES = 8
MXU_WIDTH = 256
TQ = 512
NC = 2 * MXU_WIDTH
HEADS_PER_LANE_BLOCK = LANES // HEAD_DIM
GROUP = N_Q_HEADS // N_KV_HEADS
LANE_BLOCKS_PER_GROUP = GROUP // HEADS_PER_LANE_BLOCK
ATTN_LAG = 2
VMEM_LIMIT_BYTES = 60 * 1024 * 1024

SLOT_TO_HEAD_IN_GROUP = tuple(
    lb * HEADS_PER_LANE_BLOCK + half
    for half in range(HEADS_PER_LANE_BLOCK) for lb in range(LANE_BLOCKS_PER_GROUP))

_BF16 = jnp.bfloat16
_F32 = jnp.float32


def _dot(a, b):
    return jnp.dot(a, b, preferred_element_type=_F32)


def _dot_scores(q, k):
    return lax.dot_general(q, k, (((1,), (1,)), ((), ())), preferred_element_type=_F32)


def _dot_pv(p, v):
    return jnp.dot(p, v, preferred_element_type=_F32)


def _head_rms_norm(x, gain, low_half):
    sq = x * x
    ss_lo = jnp.sum(jnp.where(low_half, sq, 0.0), axis=-1, keepdims=True)
    ss_hi = jnp.sum(jnp.where(low_half, 0.0, sq), axis=-1, keepdims=True)
    r_lo = lax.rsqrt(ss_lo * (1.0 / HEAD_DIM) + EPS)
    r_hi = lax.rsqrt(ss_hi * (1.0 / HEAD_DIM) + EPS)
    return (x * jnp.where(low_half, r_lo, r_hi)) * gain


N_WEIGHTS = 4


def _layer_kernel(layer, cast_next, sinks_ref, x_ref, ng_ref, convw_ref, qg_ref, kg_ref, gb_ref,
                  win_ref, wco_ref, wao_ref, wout_ref, *refs):
    n_cast = N_WEIGHTS if cast_next else 0
    next_f32_refs, refs = refs[:n_cast], refs[n_cast:]
    out_ref, refs = refs[0], refs[1:]
    next_bf16_refs, refs = refs[:n_cast], refs[n_cast:]
    (h_ref, cvbuf_ref, cvcarry_ref, yc_ref, qn_ref,
     kk_ref, vlo_ref, vhi_ref, za_ref, o_ref, gateb_ref) = refs
    gya_ref = out_ref
    t = pl.program_id(1)
    tq = x_ref.shape[1]
    n_blocks = tq // BLOCK

    for src_ref, dst_ref in zip(next_f32_refs, next_bf16_refs):
        dst_ref[...] = src_ref[...].astype(_BF16)

    @pl.when(t == 0)
    def _():
        cvcarry_ref[...] = jnp.zeros_like(cvcarry_ref)
        kk_ref[:, 0:BLOCK, :] = jnp.zeros((N_KV_HEADS, BLOCK, LANES), _BF16)
        vlo_ref[:, 0:BLOCK, :] = jnp.zeros((N_KV_HEADS, BLOCK, LANES), _BF16)
        vhi_ref[:, 0:BLOCK, :] = jnp.zeros((N_KV_HEADS, BLOCK, LANES), _BF16)

    x = x_ref[0]
    ms = jnp.mean(x * x, axis=-1, keepdims=True)
    h_ref[...] = ((x * lax.rsqrt(ms + EPS)) * ng_ref[...]).astype(_BF16)

    def project(col, width=NC):
        return _dot(h_ref[...], win_ref[:, col:col + width])

    lane = lax.broadcasted_iota(jnp.int32, (tq, LANES), 1)
    low_half = lane < HEAD_DIM
    scale = HEAD_DIM ** -0.5
    groups_per_matmul = NC // (GROUP * HEAD_DIM)
    q_issued, za_issued = set(), set()

    def project_q(n):
        q_issued.add(n)
        q = project(OFF_Q + n * NC)
        for j in range(NC // LANES):
            m = n * (NC // LANES) + j
            qs = _head_rms_norm(q[:, j * LANES:(j + 1) * LANES], qg_ref[...], low_half) * scale
            qn_ref[:, m * LANES:(m + 1) * LANES] = qs.astype(_BF16)

    project_q(0)

    assert OFF_V == OFF_K + KV_WIDTH
    kv = project(OFF_K, 2 * KV_WIDTH)
    cur = slice(BLOCK, BLOCK + tq)
    for mk in range(KV_WIDTH // LANES):
        kp = _head_rms_norm(kv[:, mk * LANES:(mk + 1) * LANES], kg_ref[...], low_half)
        kr = pltpu.roll(kp, HEAD_DIM, axis=1)
        vp = kv[:, KV_WIDTH + mk * LANES:KV_WIDTH + (mk + 1) * LANES]
        vr = pltpu.roll(vp, HEAD_DIM, axis=1)
        g_even = mk * HEADS_PER_LANE_BLOCK
        g_odd = g_even + 1
        kk_ref[g_even, cur, :] = jnp.where(low_half, kp, kr).astype(_BF16)
        kk_ref[g_odd, cur, :] = jnp.where(low_half, kr, kp).astype(_BF16)
        vlo_ref[g_even, cur, :] = jnp.where(low_half, vp, 0.0).astype(_BF16)
        vhi_ref[g_even, cur, :] = jnp.where(low_half, 0.0, vr).astype(_BF16)
        vlo_ref[g_odd, cur, :] = jnp.where(low_half, vr, 0.0).astype(_BF16)
        vhi_ref[g_odd, cur, :] = jnp.where(low_half, 0.0, vp).astype(_BF16)

    buf = jnp.minimum(t, 0)

    def project_za(n):
        za_issued.add(n)
        za_ref[buf, :, n * NC:(n + 1) * NC] = jax.nn.silu(project(OFF_ZA + n * NC))

    project_za(0)

    def independent_stages():
        for n in range(1, ATTN_WIDTH // NC):
            project_q(n)
            yield
            project_za(n)
            yield
        for n in range(D_MODEL // NC):
            cols = slice(n * NC, (n + 1) * NC)
            gateb_ref[buf,:, cols] = jax.nn.sigmoid(
                project(OFF_GATE + D_MODEL + n * NC) + gb_ref[:, D_MODEL + n * NC:D_MODEL + (n + 1) * NC])
            yield
        for n in range(D_MODEL // NC):
            cols = slice(n * NC, (n + 1) * NC)
            gya_ref[buf,:, cols] = jax.nn.sigmoid(project(OFF_GATE + n * NC) + gb_ref[:, cols])
            yield
        for c in range(CONV_WIDTH // NC):
            cols = slice(c * NC, (c + 1) * NC)
            v_c = project(OFF_VC + c * NC)
            yield
            cv = project(OFF_CC + c * NC) * v_c
            cvbuf_ref[0:SUBLANES, :] = cvcarry_ref[:, cols]
            cvbuf_ref[SUBLANES:SUBLANES + tq, :] = cv
            cvcarry_ref[:, cols] = cv[tq - SUBLANES:tq, :]
            w = convw_ref[:, cols]
            y = cvbuf_ref[SUBLANES - 2:SUBLANES - 2 + tq, :] * w[0:1, :]
            y = y + cvbuf_ref[SUBLANES - 1:SUBLANES - 1 + tq, :] * w[1:2, :]
            y = y + cv * w[2:3, :]
            yield
            by = project(OFF_BC + c * NC) * y
            yield
            yc_ref[:, cols] = (by * jax.nn.silu(project(OFF_ZC + c * NC))).astype(_BF16)
            yield
        for n in range(D_MODEL // NC):
            cols = slice(n * NC, (n + 1) * NC)
            gya_ref[buf,:, cols] = gya_ref[buf,:, cols] * _dot(yc_ref[...], wco_ref[:, cols])
            yield

    qi = lax.broadcasted_iota(jnp.int32, (BLOCK, 2 * BLOCK), 0)
    kc = lax.broadcasted_iota(jnp.int32, (BLOCK, 2 * BLOCK), 1)
    dist = kc - qi
    band = (dist > 0) & (dist <= WINDOW)
    band_first = band & ((kc >= BLOCK) | (t > 0))
    low_half_block = lax.broadcasted_iota(jnp.int32, (BLOCK, LANES), 1) < HEAD_DIM
    low_half_pair = lax.broadcasted_iota(
        jnp.int32, (LANE_BLOCKS_PER_GROUP * BLOCK, LANES), 1) < HEAD_DIM

    def softmax_unnormalised(s, valid, sink):
        s = jnp.where(valid, s, NEG_INF)
        m = jnp.maximum(jnp.max(s, axis=-1, keepdims=True), sink)
        p = jnp.exp(s - m)
        den = jnp.sum(p, axis=-1, keepdims=True) + jnp.exp(sink - m)
        return p.astype(_BF16), den

    def scores(b, g):
        assert g // groups_per_matmul in q_issued
        rows = slice(b * BLOCK, (b + 1) * BLOCK)
        win = slice(b * BLOCK, b * BLOCK + 2 * BLOCK)
        blocks = [qn_ref[rows, (g * LANE_BLOCKS_PER_GROUP + i) * LANES:(g * LANE_BLOCKS_PER_GROUP + i + 1) * LANES]
                  for i in range(LANE_BLOCKS_PER_GROUP)]
        zero = jnp.zeros((BLOCK, LANES), _BF16)
        q_stack = jnp.concatenate([jnp.where(low_half_block, blk, zero) for blk in blocks]
                                  + [jnp.where(low_half_block, zero, blk) for blk in blocks], axis=0)
        return _dot_scores(q_stack, kk_ref[g, win, :])

    def attend(b, g, s):
        assert g // groups_per_matmul in za_issued
        rows = slice(b * BLOCK, (b + 1) * BLOCK)
        win = slice(b * BLOCK, b * BLOCK + 2 * BLOCK)
        valid = band_first if b == 0 else band
        ps, dens = [], []
        for slot in range(GROUP):
            head = g * GROUP + SLOT_TO_HEAD_IN_GROUP[slot]
            p, den = softmax_unnormalised(s[slot * BLOCK:(slot + 1) * BLOCK], valid, sinks_ref[layer, head])
            ps.append(p)
            dens.append(den)
        half_rows = LANE_BLOCKS_PER_GROUP
        p_low = jnp.concatenate(ps[:half_rows], axis=0)
        p_high = jnp.concatenate(ps[half_rows:], axis=0)
        o = _dot_pv(p_low, vlo_ref[g, win, :]) + _dot_pv(p_high, vhi_ref[g, win, :])
        den = jnp.where(low_half_pair,
                        jnp.concatenate(dens[:half_rows], axis=0),
                        jnp.concatenate(dens[half_rows:], axis=0))
        o = o / den
        for i in range(LANE_BLOCKS_PER_GROUP):
            lb = slice((g * LANE_BLOCKS_PER_GROUP + i) * LANES, (g * LANE_BLOCKS_PER_GROUP + i + 1) * LANES)
            o_ref[buf,rows, lb] = (o[i * BLOCK:(i + 1) * BLOCK] * za_ref[buf,rows, lb]).astype(_BF16)

    steps = [(b, g) for g in range(N_KV_HEADS) for b in range(n_blocks)]
    independent = independent_stages()
    in_flight = {}
    for i in range(len(steps) + ATTN_LAG):
        if i < len(steps):
            in_flight[i] = scores(*steps[i])
        if i >= ATTN_LAG:
            next(independent, None)
            attend(*steps[i - ATTN_LAG], in_flight.pop(i - ATTN_LAG))
    for _ in independent:
        pass

    last = slice(tq, tq + BLOCK)
    kk_ref[:, 0:BLOCK, :] = kk_ref[:, last, :]
    vlo_ref[:, 0:BLOCK, :] = vlo_ref[:, last, :]
    vhi_ref[:, 0:BLOCK, :] = vhi_ref[:, last, :]

    merged = []
    for n in range(D_MODEL // NC):
        cols = slice(n * NC, (n + 1) * NC)
        y_b = _dot(o_ref[buf], wao_ref[:, cols])
        merged.append((gya_ref[buf,:, cols] + gateb_ref[buf,:, cols] * y_b).astype(_BF16))
    y = x_ref[0]
    for n in range(D_MODEL // NC):
        y = y + _dot(merged[n], wout_ref[n * NC:(n + 1) * NC, :])
    out_ref[0] = y


def _layer_resident(layer, shape):
    zeros = (0,) * len(shape)
    return pl.BlockSpec((None,) + shape, lambda b, t: (layer,) + zeros, pipeline_mode=pl.Buffered(1))


def _resident(shape):
    return pl.BlockSpec(shape, lambda b, t: (0,) * len(shape), pipeline_mode=pl.Buffered(1))


def _hybrid_layer(layer, tq, x, sinks, small_params, weights, next_weights_f32):
    batch, seq, d = x.shape
    assert d == D_MODEL and seq % tq == 0 and tq % BLOCK == 0
    assert len(weights) == N_WEIGHTS and weights[0].shape == (D_MODEL, IN_COLS)
    steps_per_seq = seq // tq
    grid = (batch, steps_per_seq)
    kv_rows = tq + BLOCK
    cast_next = next_weights_f32 is not None

    in_specs = [
        pl.BlockSpec(memory_space=pltpu.SMEM),
        pl.BlockSpec((1, tq, D_MODEL), lambda b, t: (b, t, 0)),
        _layer_resident(layer, (1, D_MODEL)),
        _layer_resident(layer, (CONV_K, CONV_WIDTH)),
        _layer_resident(layer, (1, LANES)),
        _layer_resident(layer, (1, LANES)),
        _layer_resident(layer, (1, N_BRANCHES * D_MODEL)),
    ] + [_resident(w.shape) for w in weights]
    out_specs = [pl.BlockSpec((1, tq, D_MODEL), lambda b, t: (b, t, 0))]
    out_shape = [jax.ShapeDtypeStruct(x.shape, x.dtype)]
    operands = [sinks, x, *small_params, *weights]
    if cast_next:
        n_steps = batch * steps_per_seq
        for w in next_weights_f32:
            rows, cols = w.shape[1:]
            assert rows % n_steps == 0
            rows_per_step = rows // n_steps
            in_specs.append(pl.BlockSpec((None, rows_per_step, cols),
                                         lambda b, t: (layer + 1, b * steps_per_seq + t, 0)))
            out_specs.append(pl.BlockSpec((rows_per_step, cols), lambda b, t: (b * steps_per_seq + t, 0)))
            out_shape.append(jax.ShapeDtypeStruct((rows, cols), _BF16))
            operands.append(w)

    outs = pl.pallas_call(
        functools.partial(_layer_kernel, layer, cast_next),
        grid=grid,
        in_specs=in_specs,
        out_specs=out_specs,
        out_shape=out_shape,
        scratch_shapes=[
            pltpu.VMEM((tq, D_MODEL), _BF16),
            pltpu.VMEM((tq + SUBLANES, NC), _F32),
            pltpu.VMEM((SUBLANES, CONV_WIDTH), _F32),
            pltpu.VMEM((tq, CONV_WIDTH), _BF16),
            pltpu.VMEM((tq, ATTN_WIDTH), _BF16),
            pltpu.VMEM((N_KV_HEADS, kv_rows, LANES), _BF16),
            pltpu.VMEM((N_KV_HEADS, kv_rows, LANES), _BF16),
            pltpu.VMEM((N_KV_HEADS, kv_rows, LANES), _BF16),
            pltpu.VMEM((1, tq, ATTN_WIDTH), _F32),
            pltpu.VMEM((1, tq, ATTN_WIDTH), _BF16),
            pltpu.VMEM((1, tq, D_MODEL), _F32),
        ],
        compiler_params=pltpu.CompilerParams(
            dimension_semantics=("arbitrary", "arbitrary"),
            vmem_limit_bytes=VMEM_LIMIT_BYTES,
        ),
        name=f"hybrid_layer_{layer}",
    )(*operands)
    return outs[0], tuple(outs[1:])


def kernel(x, norm_g, w_in, conv_w, q_norm_g, k_norm_g, sinks, w_conv_out, w_attn_out, gate_b, w_out):
    depth = norm_g.shape[0]
    tq = min(TQ, x.shape[1])
    small_params = (
        norm_g[:, None, :],
        conv_w,
        jnp.tile(q_norm_g, (1, HEADS_PER_LANE_BLOCK))[:, None, :],
        jnp.tile(k_norm_g, (1, HEADS_PER_LANE_BLOCK))[:, None, :],
        gate_b[:, None, :],
    )
    weights_f32 = (w_in, w_conv_out, w_attn_out, w_out)
    weights = tuple(w[0].astype(_BF16) for w in weights_f32)
    for layer in range(depth):
        x, weights = _hybrid_layer(layer, tq, x, sinks, small_params, weights,
                                   weights_f32 if layer + 1 < depth else None)
    return x
```

```python
import functools

import jax
import jax.numpy as jnp
from jax import lax
from jax.experimental import pallas as pl
from jax.experimental.pallas import tpu as pltpu

D_MODEL = 1024
CONV_WIDTH = D_MODEL
CONV_K = 3
N_Q_HEADS = 16
N_KV_HEADS = 4
HEAD_DIM = 64
ATTN_WIDTH = N_Q_HEADS * HEAD_DIM
KV_WIDTH = N_KV_HEADS * HEAD_DIM
WINDOW = 128
BLOCK = 128
N_BRANCHES = 2
EPS = 1e-6
NEG_INF = -1e30
LOG2_E = 1.4426950408889634

OFF_VC = 0
OFF_BC = OFF_VC + CONV_WIDTH
OFF_CC = OFF_BC + CONV_WIDTH
OFF_ZC = OFF_CC + CONV_WIDTH
OFF_Q = OFF_ZC + CONV_WIDTH
OFF_K = OFF_Q + ATTN_WIDTH
OFF_V = OFF_K + KV_WIDTH
OFF_ZA = OFF_V + KV_WIDTH
OFF_GATE = OFF_ZA + ATTN_WIDTH
IN_COLS = OFF_GATE + N_BRANCHES * D_MODEL

LANES = 128
SUBLANES = 8
MXU_WIDTH = 256
TQ = 512
NC = 2 * MXU_WIDTH
HEADS_PER_LANE_BLOCK = LANES // HEAD_DIM
GROUP = N_Q_HEADS // N_KV_HEADS
LANE_BLOCKS_PER_GROUP = GROUP // HEADS_PER_LANE_BLOCK
ATTN_LAG = 2
VMEM_LIMIT_BYTES = 60 * 1024 * 1024

SLOT_TO_HEAD_IN_GROUP = tuple(
    lb * HEADS_PER_LANE_BLOCK + half
    for half in range(HEADS_PER_LANE_BLOCK) for lb in range(LANE_BLOCKS_PER_GROUP))

_BF16 = jnp.bfloat16
_F32 = jnp.float32


def _dot(a, b):
    return jnp.dot(a, b, preferred_element_type=_F32)


def _dot_scores(q, k):
    return lax.dot_general(q, k, (((1,), (1,)), ((), ())), preferred_element_type=_F32)


def _dot_pv(p, v):
    return jnp.dot(p, v, preferred_element_type=_F32)


def _head_rms_norm(x, gain, low_half):
    sq = x * x
    ss_lo = jnp.sum(jnp.where(low_half, sq, 0.0), axis=-1, keepdims=True)
    ss_hi = jnp.sum(jnp.where(low_half, 0.0, sq), axis=-1, keepdims=True)
    r_lo = lax.rsqrt(ss_lo * (1.0 / HEAD_DIM) + EPS)
    r_hi = lax.rsqrt(ss_hi * (1.0 / HEAD_DIM) + EPS)
    return (x * jnp.where(low_half, r_lo, r_hi)) * gain


N_WEIGHTS = 4


def _layer_kernel(layer, cast_next, sinks_ref, x_ref, ng_ref, convw_ref, qg_ref, kg_ref, gb_ref,
                  win_ref, wco_ref, wao_ref, wout_ref, *refs):
    n_cast = N_WEIGHTS if cast_next else 0
    next_f32_refs, refs = refs[:n_cast], refs[n_cast:]
    out_ref, refs = refs[0], refs[1:]
    next_bf16_refs, refs = refs[:n_cast], refs[n_cast:]
    (h_ref, cvcarry_ref, yc_ref, qn_ref,
     kk_ref, vlo_ref, vhi_ref, za_ref, o_ref, gateb_ref) = refs
    gya_ref = out_ref
    t = pl.program_id(1)
    tq = x_ref.shape[1]
    n_blocks = tq // BLOCK

    for src_ref, dst_ref in zip(next_f32_refs, next_bf16_refs):
        dst_ref[...] = src_ref[...].astype(_BF16)

    @pl.when(t == 0)
    def _():
        cvcarry_ref[...] = jnp.zeros_like(cvcarry_ref)
        kk_ref[:, 0:BLOCK, :] = jnp.zeros((N_KV_HEADS, BLOCK, LANES), _BF16)
        vlo_ref[:, 0:BLOCK, :] = jnp.zeros((N_KV_HEADS, BLOCK, LANES), _BF16)
        vhi_ref[:, 0:BLOCK, :] = jnp.zeros((N_KV_HEADS, BLOCK, LANES), _BF16)

    x = x_ref[0]
    ms = jnp.mean(x * x, axis=-1, keepdims=True)
    inv_rms = lax.rsqrt(ms + EPS)
    h_ref[...] = ((x * inv_rms) * ng_ref[...]).astype(_BF16)
    xg = (x * ng_ref[...]).astype(_BF16)

    def project(col, width=NC):
        return _dot(h_ref[...], win_ref[:, col:col + width])

    lane = lax.broadcasted_iota(jnp.int32, (tq, LANES), 1)
    low_half = lane < HEAD_DIM
    scale = HEAD_DIM ** -0.5 * LOG2_E
    groups_per_matmul = NC // (GROUP * HEAD_DIM)
    q_issued, za_issued = set(), set()

    def project_q(n, first_matmul=False):
        q_issued.add(n)
        if first_matmul:
            q = _dot(xg, win_ref[:, OFF_Q + n * NC:OFF_Q + (n + 1) * NC]) * inv_rms
        else:
            q = project(OFF_Q + n * NC)
        for j in range(NC // LANES):
            m = n * (NC // LANES) + j
            qs = _head_rms_norm(q[:, j * LANES:(j + 1) * LANES], qg_ref[...], low_half) * scale
            qn_ref[:, m * LANES:(m + 1) * LANES] = qs.astype(_BF16)

    project_q(0, first_matmul=True)

    assert OFF_V == OFF_K + KV_WIDTH
    kv = project(OFF_K, 2 * KV_WIDTH)
    cur = slice(BLOCK, BLOCK + tq)
    for mk in range(KV_WIDTH // LANES):
        kp = _head_rms_norm(kv[:, mk * LANES:(mk + 1) * LANES], kg_ref[...], low_half)
        kr = pltpu.roll(kp, HEAD_DIM, axis=1)
        vp = kv[:, KV_WIDTH + mk * LANES:KV_WIDTH + (mk + 1) * LANES]
        vr = pltpu.roll(vp, HEAD_DIM, axis=1)
        g_even = mk * HEADS_PER_LANE_BLOCK
        g_odd = g_even + 1
        kk_ref[g_even, cur, :] = jnp.where(low_half, kp, kr).astype(_BF16)
        kk_ref[g_odd, cur, :] = jnp.where(low_half, kr, kp).astype(_BF16)
        vlo_ref[g_even, cur, :] = jnp.where(low_half, vp, 0.0).astype(_BF16)
        vhi_ref[g_even, cur, :] = jnp.where(low_half, 0.0, vr).astype(_BF16)
        vlo_ref[g_odd, cur, :] = jnp.where(low_half, vr, 0.0).astype(_BF16)
        vhi_ref[g_odd, cur, :] = jnp.where(low_half, 0.0, vp).astype(_BF16)

    buf = jnp.minimum(t, 0)

    def project_za(n):
        za_issued.add(n)
        za_ref[buf, :, n * NC:(n + 1) * NC] = jax.nn.silu(project(OFF_ZA + n * NC))

    project_za(0)

    row8 = lax.broadcasted_iota(jnp.int32, (SUBLANES, NC), 0)

    def independent_stages():
        for n in range(1, ATTN_WIDTH // NC):
            project_q(n)
            yield
            project_za(n)
            yield
        assert CONV_WIDTH == D_MODEL
        for c in range(CONV_WIDTH // NC):
            cols = slice(c * NC, (c + 1) * NC)
            v_c = project(OFF_VC + c * NC)
            yield
            cv = project(OFF_CC + c * NC) * v_c
            carry = cvcarry_ref[:, cols]
            cvcarry_ref[:, cols] = cv[tq - SUBLANES:tq, :]

            def delayed(k):
                r = pltpu.roll(cv, k, axis=0)
                head = jnp.where(row8 < k, pltpu.roll(carry, k, axis=0), r[0:SUBLANES, :])
                return jnp.concatenate([head, r[SUBLANES:, :]], axis=0)

            w = convw_ref[:, cols]
            y = delayed(CONV_K - 1) * w[0:1, :]
            for tap in range(1, CONV_K - 1):
                y = y + delayed(CONV_K - 1 - tap) * w[tap:tap + 1, :]
            y = y + cv * w[CONV_K - 1:CONV_K, :]
            yield
            gateb_ref[buf, :, cols] = jax.nn.sigmoid(
                project(OFF_GATE + D_MODEL + c * NC) + gb_ref[:, D_MODEL + c * NC:D_MODEL + (c + 1) * NC])
            yield
            by = project(OFF_BC + c * NC) * y
            yield
            yc_ref[:, cols] = (by * jax.nn.silu(project(OFF_ZC + c * NC))).astype(_BF16)
            yield
            gya_ref[buf, :, cols] = jax.nn.sigmoid(project(OFF_GATE + c * NC) + gb_ref[:, cols])
            yield
        for n in range(D_MODEL // NC):
            cols = slice(n * NC, (n + 1) * NC)
            gya_ref[buf,:, cols] = gya_ref[buf,:, cols] * _dot(yc_ref[...], wco_ref[:, cols])
            yield

    qi = lax.broadcasted_iota(jnp.int32, (BLOCK, 2 * BLOCK), 0)
    kc = lax.broadcasted_iota(jnp.int32, (BLOCK, 2 * BLOCK), 1)
    dist = kc - qi
    band = (dist > 0) & (dist <= WINDOW)
    band_first = band & ((kc >= BLOCK) | (t > 0))
    low_half_block = lax.broadcasted_iota(jnp.int32, (BLOCK, LANES), 1) < HEAD_DIM
    low_half_pair = lax.broadcasted_iota(
        jnp.int32, (LANE_BLOCKS_PER_GROUP * BLOCK, LANES), 1) < HEAD_DIM

    def softmax_unnormalised(s, valid, sink):
        s = jnp.where(valid, s, NEG_INF)
        m = jnp.maximum(jnp.max(s, axis=-1, keepdims=True), sink)
        p = jnp.exp2(s - m)
        den = jnp.sum(p, axis=-1, keepdims=True) + jnp.exp2(sink - m)
        return p.astype(_BF16), den

    def scores(b, g):
        assert g // groups_per_matmul in q_issued
        rows = slice(b * BLOCK, (b + 1) * BLOCK)
        win = slice(b * BLOCK, b * BLOCK + 2 * BLOCK)
        blocks = [qn_ref[rows, (g * LANE_BLOCKS_PER_GROUP + i) * LANES:(g * LANE_BLOCKS_PER_GROUP + i + 1) * LANES]
                  for i in range(LANE_BLOCKS_PER_GROUP)]
        zero = jnp.zeros((BLOCK, LANES), _BF16)
        q_stack = jnp.concatenate([jnp.where(low_half_block, blk, zero) for blk in blocks]
                                  + [jnp.where(low_half_block, zero, blk) for blk in blocks], axis=0)
        return _dot_scores(q_stack, kk_ref[g, win, :])

    def attend(b, g, s):
        assert g // groups_per_matmul in za_issued
        rows = slice(b * BLOCK, (b + 1) * BLOCK)
        win = slice(b * BLOCK, b * BLOCK + 2 * BLOCK)
        valid = band_first if b == 0 else band
        ps, dens = [], []
        for slot in range(GROUP):
            head = g * GROUP + SLOT_TO_HEAD_IN_GROUP[slot]
            p, den = softmax_unnormalised(s[slot * BLOCK:(slot + 1) * BLOCK], valid,
                                          sinks_ref[layer, head] * LOG2_E)
            ps.append(p)
            dens.append(den)
        half_rows = LANE_BLOCKS_PER_GROUP
        p_low = jnp.concatenate(ps[:half_rows], axis=0)
        p_high = jnp.concatenate(ps[half_rows:], axis=0)
        o = _dot_pv(p_low, vlo_ref[g, win, :]) + _dot_pv(p_high, vhi_ref[g, win, :])
        den = jnp.where(low_half_pair,
                        jnp.concatenate(dens[:half_rows], axis=0),
                        jnp.concatenate(dens[half_rows:], axis=0))
        o = o / den
        for i in range(LANE_BLOCKS_PER_GROUP):
            lb = slice((g * LANE_BLOCKS_PER_GROUP + i) * LANES, (g * LANE_BLOCKS_PER_GROUP + i + 1) * LANES)
            o_ref[buf,rows, lb] = (o[i * BLOCK:(i + 1) * BLOCK] * za_ref[buf,rows, lb]).astype(_BF16)

    steps = [(b, g) for g in range(N_KV_HEADS) for b in range(n_blocks)]
    independent = independent_stages()
    in_flight = {}
    for i in range(len(steps) + ATTN_LAG):
        if i < len(steps):
            in_flight[i] = scores(*steps[i])
        if i >= ATTN_LAG:
            next(independent, None)
            attend(*steps[i - ATTN_LAG], in_flight.pop(i - ATTN_LAG))
    for _ in independent:
        pass

    last = slice(tq, tq + BLOCK)
    kk_ref[:, 0:BLOCK, :] = kk_ref[:, last, :]
    vlo_ref[:, 0:BLOCK, :] = vlo_ref[:, last, :]
    vhi_ref[:, 0:BLOCK, :] = vhi_ref[:, last, :]

    merged = []
    for n in range(D_MODEL // NC):
        cols = slice(n * NC, (n + 1) * NC)
        y_b = _dot(o_ref[buf], wao_ref[:, cols])
        merged.append((gya_ref[buf,:, cols] + gateb_ref[buf,:, cols] * y_b).astype(_BF16))
    y = x_ref[0]
    for n in range(D_MODEL // NC):
        y = y + _dot(merged[n], wout_ref[n * NC:(n + 1) * NC, :])
    out_ref[0] = y


def _layer_resident(layer, shape):
    zeros = (0,) * len(shape)
    return pl.BlockSpec((None,) + shape, lambda b, t: (layer,) + zeros, pipeline_mode=pl.Buffered(1))


def _resident(shape):
    return pl.BlockSpec(shape, lambda b, t: (0,) * len(shape), pipeline_mode=pl.Buffered(1))


def _hybrid_layer(layer, tq, x, sinks, small_params, weights, next_weights_f32):
    batch, seq, d = x.shape
    assert d == D_MODEL and seq % tq == 0 and tq % BLOCK == 0
    assert len(weights) == N_WEIGHTS and weights[0].shape == (D_MODEL, IN_COLS)
    steps_per_seq = seq // tq
    grid = (batch, steps_per_seq)
    kv_rows = tq + BLOCK
    cast_next = next_weights_f32 is not None

    in_specs = [
        pl.BlockSpec(memory_space=pltpu.SMEM),
        pl.BlockSpec((1, tq, D_MODEL), lambda b, t: (b, t, 0)),
        _layer_resident(layer, (1, D_MODEL)),
        _layer_resident(layer, (CONV_K, CONV_WIDTH)),
        _layer_resident(layer, (1, LANES)),
        _layer_resident(layer, (1, LANES)),
        _layer_resident(layer, (1, N_BRANCHES * D_MODEL)),
    ] + [_resident(w.shape) for w in weights]
    out_specs = [pl.BlockSpec((1, tq, D_MODEL), lambda b, t: (b, t, 0))]
    out_shape = [jax.ShapeDtypeStruct(x.shape, x.dtype)]
    operands = [sinks, x, *small_params, *weights]
    if cast_next:
        n_steps = batch * steps_per_seq
        for w in next_weights_f32:
            rows, cols = w.shape[1:]
            assert rows % n_steps == 0
            rows_per_step = rows // n_steps
            in_specs.append(pl.BlockSpec((None, rows_per_step, cols),
                                         lambda b, t: (layer + 1, b * steps_per_seq + t, 0)))
            out_specs.append(pl.BlockSpec((rows_per_step, cols), lambda b, t: (b * steps_per_seq + t, 0)))
            out_shape.append(jax.ShapeDtypeStruct((rows, cols), _BF16))
            operands.append(w)

    outs = pl.pallas_call(
        functools.partial(_layer_kernel, layer, cast_next),
        grid=grid,
        in_specs=in_specs,
        out_specs=out_specs,
        out_shape=out_shape,
        scratch_shapes=[
            pltpu.VMEM((tq, D_MODEL), _BF16),
            pltpu.VMEM((SUBLANES, CONV_WIDTH), _F32),
            pltpu.VMEM((tq, CONV_WIDTH), _BF16),
            pltpu.VMEM((tq, ATTN_WIDTH), _BF16),
            pltpu.VMEM((N_KV_HEADS, kv_rows, LANES), _BF16),
            pltpu.VMEM((N_KV_HEADS, kv_rows, LANES), _BF16),
            pltpu.VMEM((N_KV_HEADS, kv_rows, LANES), _BF16),
            pltpu.VMEM((1, tq, ATTN_WIDTH), _F32),
            pltpu.VMEM((1, tq, ATTN_WIDTH), _BF16),
            pltpu.VMEM((1, tq, D_MODEL), _F32),
        ],
        compiler_params=pltpu.CompilerParams(
            dimension_semantics=("arbitrary", "arbitrary"),
            vmem_limit_bytes=VMEM_LIMIT_BYTES,
        ),
        name=f"hybrid_layer_{layer}",
    )(*operands)
    return outs[0], tuple(outs[1:])


def kernel(x, norm_g, w_in, conv_w, q_norm_g, k_norm_g, sinks, w_conv_out, w_attn_out, gate_b, w_out):
    depth = norm_g.shape[0]
    tq = min(TQ, x.shape[1])
    small_params = (
        norm_g[:, None, :],
        conv_w,
        jnp.tile(q_norm_g, (1, HEADS_PER_LANE_BLOCK))[:, None, :],
        jnp.tile(k_norm_g, (1, HEADS_PER_LANE_BLOCK))[:, None, :],
        gate_b[:, None, :],
    )
    weights_f32 = (w_in, w_conv_out, w_attn_out, w_out)
    weights = tuple(w[0].astype(_BF16) for w in weights_f32)
    for layer in range(depth):
        x, weights = _hybrid_layer(layer, tq, x, sinks, small_params, weights,
                                   weights_f32 if layer + 1 < depth else None)
    return x
```

```python
import functools

import jax
import jax.numpy as jnp
from jax import lax
from jax.experimental import pallas as pl
from jax.experimental.pallas import tpu as pltpu

D_MODEL = 1024
CONV_WIDTH = D_MODEL
CONV_K = 3
N_Q_HEADS = 16
N_KV_HEADS = 4
HEAD_DIM = 64
ATTN_WIDTH = N_Q_HEADS * HEAD_DIM
KV_WIDTH = N_KV_HEADS * HEAD_DIM
WINDOW = 128
BLOCK = 128
N_BRANCHES = 2
EPS = 1e-6
NEG_INF = -1e30
LOG2_E = 1.4426950408889634

OFF_VC = 0
OFF_BC = OFF_VC + CONV_WIDTH
OFF_CC = OFF_BC + CONV_WIDTH
OFF_ZC = OFF_CC + CONV_WIDTH
OFF_Q = OFF_ZC + CONV_WIDTH
OFF_K = OFF_Q + ATTN_WIDTH
OFF_V = OFF_K + KV_WIDTH
OFF_ZA = OFF_V + KV_WIDTH
OFF_GATE = OFF_ZA + ATTN_WIDTH
IN_COLS = OFF_GATE + N_BRANCHES * D_MODEL

LANES = 128
SUBLANES = 8
MXU_WIDTH = 256
TQ = 512
NC = 2 * MXU_WIDTH
HEADS_PER_LANE_BLOCK = LANES // HEAD_DIM
GROUP = N_Q_HEADS // N_KV_HEADS
LANE_BLOCKS_PER_GROUP = GROUP // HEADS_PER_LANE_BLOCK
ATTN_LAG = 2
VMEM_LIMIT_BYTES = 60 * 1024 * 1024

SLOT_TO_HEAD_IN_GROUP = tuple(
    lb * HEADS_PER_LANE_BLOCK + half
    for half in range(HEADS_PER_LANE_BLOCK) for lb in range(LANE_BLOCKS_PER_GROUP))

_BF16 = jnp.bfloat16
_F32 = jnp.float32


def _dot(a, b):
    return jnp.dot(a, b, preferred_element_type=_F32)


def _dot_scores(q, k):
    return lax.dot_general(q, k, (((1,), (1,)), ((), ())), preferred_element_type=_F32)


def _dot_pv(p, v):
    return jnp.dot(p, v, preferred_element_type=_F32)


def _head_rms_norm(x, gain, low_half):
    sq = x * x
    ss_lo = jnp.sum(jnp.where(low_half, sq, 0.0), axis=-1, keepdims=True)
    ss_hi = jnp.sum(jnp.where(low_half, 0.0, sq), axis=-1, keepdims=True)
    r_lo = lax.rsqrt(ss_lo * (1.0 / HEAD_DIM) + EPS)
    r_hi = lax.rsqrt(ss_hi * (1.0 / HEAD_DIM) + EPS)
    return (x * jnp.where(low_half, r_lo, r_hi)) * gain


N_WEIGHTS = 4


def _layer_kernel(layer, cast_next, sinks_ref, x_ref, ng_ref, convw_ref, qg_ref, kg_ref, gb_ref,
                  win_ref, wco_ref, wao_ref, wout_ref, *refs):
    n_cast = N_WEIGHTS if cast_next else 0
    next_f32_refs, refs = refs[:n_cast], refs[n_cast:]
    out_ref, refs = refs[0], refs[1:]
    next_bf16_refs, refs = refs[:n_cast], refs[n_cast:]
    (h_ref, cvcarry_ref, yc_ref, qn_ref,
     kk_ref, vlo_ref, vhi_ref, za_ref, o_ref, gateb_ref) = refs
    gya_ref = out_ref.at[0]
    t = pl.program_id(1)
    tq = x_ref.shape[1]
    n_blocks = tq // BLOCK

    @pl.when(t == 0)
    def _():
        cvcarry_ref[...] = jnp.zeros_like(cvcarry_ref)
        kk_ref[:, 0:BLOCK, :] = jnp.zeros((N_KV_HEADS, BLOCK, LANES), _BF16)
        vlo_ref[:, 0:BLOCK, :] = jnp.zeros((N_KV_HEADS, BLOCK, LANES), _BF16)
        vhi_ref[:, 0:BLOCK, :] = jnp.zeros((N_KV_HEADS, BLOCK, LANES), _BF16)

    x = x_ref[0]
    ms = jnp.mean(x * x, axis=-1, keepdims=True)
    h_ref[...] = ((x * lax.rsqrt(ms + EPS)) * ng_ref[...]).astype(_BF16)

    def project(col, width=NC):
        return _dot(h_ref[...], win_ref[:, col:col + width])

    lane = lax.broadcasted_iota(jnp.int32, (tq, LANES), 1)
    low_half = lane < HEAD_DIM
    scale = HEAD_DIM ** -0.5 * LOG2_E
    groups_per_matmul = NC // (GROUP * HEAD_DIM)
    q_issued, za_issued = set(), set()

    def project_q(n):
        q_issued.add(n)
        q = project(OFF_Q + n * NC)
        for j in range(NC // LANES):
            m = n * (NC // LANES) + j
            qs = _head_rms_norm(q[:, j * LANES:(j + 1) * LANES], qg_ref[...], low_half) * scale
            qn_ref[:, m * LANES:(m + 1) * LANES] = qs.astype(_BF16)

    project_q(0)

    assert OFF_V == OFF_K + KV_WIDTH
    kv = project(OFF_K, 2 * KV_WIDTH)
    cur = slice(BLOCK, BLOCK + tq)
    for mk in range(KV_WIDTH // LANES):
        kp = _head_rms_norm(kv[:, mk * LANES:(mk + 1) * LANES], kg_ref[...], low_half)
        kr = pltpu.roll(kp, HEAD_DIM, axis=1)
        vp = kv[:, KV_WIDTH + mk * LANES:KV_WIDTH + (mk + 1) * LANES]
        vr = pltpu.roll(vp, HEAD_DIM, axis=1)
        g_even = mk * HEADS_PER_LANE_BLOCK
        g_odd = g_even + 1
        kk_ref[g_even, cur, :] = jnp.where(low_half, kp, kr).astype(_BF16)
        kk_ref[g_odd, cur, :] = jnp.where(low_half, kr, kp).astype(_BF16)
        vlo_ref[g_even, cur, :] = jnp.where(low_half, vp, 0.0).astype(_BF16)
        vhi_ref[g_even, cur, :] = jnp.where(low_half, 0.0, vr).astype(_BF16)
        vlo_ref[g_odd, cur, :] = jnp.where(low_half, vr, 0.0).astype(_BF16)
        vhi_ref[g_odd, cur, :] = jnp.where(low_half, 0.0, vp).astype(_BF16)

    def project_za(n):
        za_issued.add(n)
        za_ref[:, n * NC:(n + 1) * NC] = jax.nn.silu(project(OFF_ZA + n * NC))

    project_za(0)

    row8 = lax.broadcasted_iota(jnp.int32, (SUBLANES, MXU_WIDTH), 0)

    def merge_gate(branch, n):
        col = branch * D_MODEL + n * NC
        return jax.nn.sigmoid(project(OFF_GATE + col) + gb_ref[:, col:col + NC])

    def independent_stages():
        for n in range(1, ATTN_WIDTH // NC):
            project_q(n)
            yield
            project_za(n)
            yield
        gates = [(1, n) for n in range(D_MODEL // NC)] + [(0, n) for n in range(D_MODEL // NC)]
        for c in range(CONV_WIDTH // MXU_WIDTH):
            cols = slice(c * MXU_WIDTH, (c + 1) * MXU_WIDTH)
            cv = project(OFF_CC + c * MXU_WIDTH, MXU_WIDTH) * project(OFF_VC + c * MXU_WIDTH, MXU_WIDTH)
            carry = cvcarry_ref[:, cols]
            cvcarry_ref[:, cols] = cv[tq - SUBLANES:tq, :]

            def delayed(k):
                r = pltpu.roll(cv, k, axis=0)
                head = jnp.where(row8 < k, pltpu.roll(carry, k, axis=0), r[0:SUBLANES, :])
                return jnp.concatenate([head, r[SUBLANES:, :]], axis=0)

            w = convw_ref[:, cols]
            y = delayed(CONV_K - 1) * w[0:1, :]
            for tap in range(1, CONV_K - 1):
                y = y + delayed(CONV_K - 1 - tap) * w[tap:tap + 1, :]
            y = y + cv * w[CONV_K - 1:CONV_K, :]
            yield
            by = project(OFF_BC + c * MXU_WIDTH, MXU_WIDTH) * y
            yc_ref[:, cols] = (by * jax.nn.silu(project(OFF_ZC + c * MXU_WIDTH, MXU_WIDTH))).astype(_BF16)
            yield
            branch, n = gates.pop(0)
            gate_ref = gateb_ref if branch == 1 else gya_ref
            gate_ref[:, n * NC:(n + 1) * NC] = merge_gate(branch, n)
            yield
        assert not gates
        for n in range(D_MODEL // NC):
            cols = slice(n * NC, (n + 1) * NC)
            gya_ref[:, cols] = gya_ref[:, cols] * _dot(yc_ref[...], wco_ref[:, cols])
            yield

    qi = lax.broadcasted_iota(jnp.int32, (BLOCK, 2 * BLOCK), 0)
    kc = lax.broadcasted_iota(jnp.int32, (BLOCK, 2 * BLOCK), 1)
    dist = kc - qi
    in_band = (dist > 0) & (dist <= WINDOW)
    band = jnp.where(in_band, jnp.inf, NEG_INF)
    band_first = jnp.where(in_band & ((kc >= BLOCK) | (t > 0)), jnp.inf, NEG_INF)
    low_half_block = lax.broadcasted_iota(jnp.int32, (BLOCK, LANES), 1) < HEAD_DIM
    low_half_pair = lax.broadcasted_iota(
        jnp.int32, (LANE_BLOCKS_PER_GROUP * BLOCK, LANES), 1) < HEAD_DIM

    def softmax_unnormalised(s, cap, sink):
        s = jnp.minimum(s, cap)
        m = jnp.maximum(jnp.max(s, axis=-1, keepdims=True), sink)
        p = jnp.exp2(s - m)
        den = jnp.sum(p, axis=-1, keepdims=True) + jnp.exp2(sink - m)
        return p.astype(_BF16), den

    def scores(b, g):
        assert g // groups_per_matmul in q_issued
        rows = slice(b * BLOCK, (b + 1) * BLOCK)
        win = slice(b * BLOCK, b * BLOCK + 2 * BLOCK)
        blocks = [qn_ref[rows, (g * LANE_BLOCKS_PER_GROUP + i) * LANES:(g * LANE_BLOCKS_PER_GROUP + i + 1) * LANES]
                  for i in range(LANE_BLOCKS_PER_GROUP)]
        zero = jnp.zeros((BLOCK, LANES), _BF16)
        q_stack = jnp.concatenate([jnp.where(low_half_block, blk, zero) for blk in blocks]
                                  + [jnp.where(low_half_block, zero, blk) for blk in blocks], axis=0)
        return _dot_scores(q_stack, kk_ref[g, win, :])

    def attend(b, g, s):
        assert g // groups_per_matmul in za_issued
        rows = slice(b * BLOCK, (b + 1) * BLOCK)
        win = slice(b * BLOCK, b * BLOCK + 2 * BLOCK)
        cap = band_first if b == 0 else band
        ps, dens = [], []
        for slot in range(GROUP):
            head = g * GROUP + SLOT_TO_HEAD_IN_GROUP[slot]
            p, den = softmax_unnormalised(s[slot * BLOCK:(slot + 1) * BLOCK], cap,
                                          sinks_ref[layer, head] * LOG2_E)
            ps.append(p)
            dens.append(den)
        n_low = LANE_BLOCKS_PER_GROUP
        o2 = _dot_pv(jnp.concatenate(ps, axis=0),
                     jnp.concatenate([vlo_ref[g, win, :], vhi_ref[g, win, :]], axis=1))
        o = o2[0:n_low * BLOCK, 0:LANES] + o2[n_low * BLOCK:, LANES:]
        den = jnp.where(low_half_pair,
                        jnp.concatenate(dens[:n_low], axis=0),
                        jnp.concatenate(dens[n_low:], axis=0))
        o = o / den
        for i in range(LANE_BLOCKS_PER_GROUP):
            lb = slice((g * LANE_BLOCKS_PER_GROUP + i) * LANES, (g * LANE_BLOCKS_PER_GROUP + i + 1) * LANES)
            o_ref[rows, lb] = (o[i * BLOCK:(i + 1) * BLOCK] * za_ref[rows, lb]).astype(_BF16)

    steps = [(b, g) for g in range(N_KV_HEADS) for b in range(n_blocks)]
    independent = independent_stages()
    in_flight = {}
    for i in range(len(steps) + ATTN_LAG):
        if i < len(steps):
            in_flight[i] = scores(*steps[i])
        if i >= ATTN_LAG:
            next(independent, None)
            attend(*steps[i - ATTN_LAG], in_flight.pop(i - ATTN_LAG))
    for _ in independent:
        pass

    last = slice(tq, tq + BLOCK)
    kk_ref[:, 0:BLOCK, :] = kk_ref[:, last, :]
    vlo_ref[:, 0:BLOCK, :] = vlo_ref[:, last, :]
    vhi_ref[:, 0:BLOCK, :] = vhi_ref[:, last, :]

    merged = []
    for n in range(D_MODEL // NC):
        cols = slice(n * NC, (n + 1) * NC)
        y_b = _dot(o_ref[...], wao_ref[:, cols])
        merged.append((gya_ref[:, cols] + gateb_ref[:, cols] * y_b).astype(_BF16))
    y = x_ref[0]
    for n in range(D_MODEL // NC):
        y = y + _dot(merged[n], wout_ref[n * NC:(n + 1) * NC, :])
    out_ref[0] = y

    for src_ref, dst_ref in zip(next_f32_refs, next_bf16_refs):
        dst_ref[...] = src_ref[...].astype(_BF16)


def _layer_resident(layer, shape):
    zeros = (0,) * len(shape)
    return pl.BlockSpec((None,) + shape, lambda b, t: (layer,) + zeros, pipeline_mode=pl.Buffered(1))


def _resident(shape):
    return pl.BlockSpec(shape, lambda b, t: (0,) * len(shape), pipeline_mode=pl.Buffered(1))


def _hybrid_layer(layer, tq, x, sinks, small_params, weights, next_weights_f32):
    batch, seq, d = x.shape
    assert d == D_MODEL and seq % tq == 0 and tq % BLOCK == 0
    assert len(weights) == N_WEIGHTS and weights[0].shape == (D_MODEL, IN_COLS)
    steps_per_seq = seq // tq
    grid = (batch, steps_per_seq)
    kv_rows = tq + BLOCK
    cast_next = next_weights_f32 is not None

    in_specs = [
        pl.BlockSpec(memory_space=pltpu.SMEM),
        pl.BlockSpec((1, tq, D_MODEL), lambda b, t: (b, t, 0)),
        _layer_resident(layer, (1, D_MODEL)),
        _layer_resident(layer, (CONV_K, CONV_WIDTH)),
        _layer_resident(layer, (1, LANES)),
        _layer_resident(layer, (1, LANES)),
        _layer_resident(layer, (1, N_BRANCHES * D_MODEL)),
    ] + [_resident(w.shape) for w in weights]
    out_specs = [pl.BlockSpec((1, tq, D_MODEL), lambda b, t: (b, t, 0))]
    out_shape = [jax.ShapeDtypeStruct(x.shape, x.dtype)]
    operands = [sinks, x, *small_params, *weights]
    if cast_next:
        n_steps = batch * steps_per_seq
        for w in next_weights_f32:
            rows, cols = w.shape[1:]
            assert rows % n_steps == 0
            rows_per_step = rows // n_steps
            in_specs.append(pl.BlockSpec((None, rows_per_step, cols),
                                         lambda b, t: (layer + 1, b * steps_per_seq + t, 0)))
            out_specs.append(pl.BlockSpec((rows_per_step, cols), lambda b, t: (b * steps_per_seq + t, 0)))
            out_shape.append(jax.ShapeDtypeStruct((rows, cols), _BF16))
            operands.append(w)

    outs = pl.pallas_call(
        functools.partial(_layer_kernel, layer, cast_next),
        grid=grid,
        in_specs=in_specs,
        out_specs=out_specs,
        out_shape=out_shape,
        scratch_shapes=[
            pltpu.VMEM((tq, D_MODEL), _BF16),
            pltpu.VMEM((SUBLANES, CONV_WIDTH), _F32),
            pltpu.VMEM((tq, CONV_WIDTH), _BF16),
            pltpu.VMEM((tq, ATTN_WIDTH), _BF16),
            pltpu.VMEM((N_KV_HEADS, kv_rows, LANES), _BF16),
            pltpu.VMEM((N_KV_HEADS, kv_rows, LANES), _BF16),
            pltpu.VMEM((N_KV_HEADS, kv_rows, LANES), _BF16),
            pltpu.VMEM((tq, ATTN_WIDTH), _F32),
            pltpu.VMEM((tq, ATTN_WIDTH), _BF16),
            pltpu.VMEM((tq, D_MODEL), _F32),
        ],
        compiler_params=pltpu.CompilerParams(
            dimension_semantics=("arbitrary", "arbitrary"),
            vmem_limit_bytes=VMEM_LIMIT_BYTES,
        ),
        name=f"hybrid_layer_{layer}",
    )(*operands)
    return outs[0], tuple(outs[1:])


def kernel(x, norm_g, w_in, conv_w, q_norm_g, k_norm_g, sinks, w_conv_out, w_attn_out, gate_b, w_out):
    depth = norm_g.shape[0]
    tq = min(TQ, x.shape[1])
    small_params = (
        norm_g[:, None, :],
        conv_w,
        jnp.tile(q_norm_g, (1, HEADS_PER_LANE_BLOCK))[:, None, :],
        jnp.tile(k_norm_g, (1, HEADS_PER_LANE_BLOCK))[:, None, :],
        gate_b[:, None, :],
    )
    weights_f32 = (w_in, w_conv_out, w_attn_out, w_out)
    weights = tuple(w[0].astype(_BF16) for w in weights_f32)
    for layer in range(depth):
        x, weights = _hybrid_layer(layer, tq, x, sinks, small_params, weights,
                                   weights_f32 if layer + 1 < depth else None)
    return x
```

```python
import functools

import jax
import jax.numpy as jnp
from jax import lax
from jax.experimental import pallas as pl
from jax.experimental.pallas import tpu as pltpu

D_MODEL = 1024
CONV_WIDTH = D_MODEL
CONV_K = 3
N_Q_HEADS = 16
N_KV_HEADS = 4
HEAD_DIM = 64
ATTN_WIDTH = N_Q_HEADS * HEAD_DIM
KV_WIDTH = N_KV_HEADS * HEAD_DIM
WINDOW = 128
BLOCK = 128
N_BRANCHES = 2
EPS = 1e-6
NEG_INF = -1e30
LOG2_E = 1.4426950408889634

OFF_VC = 0
OFF_BC = OFF_VC + CONV_WIDTH
OFF_CC = OFF_BC + CONV_WIDTH
OFF_ZC = OFF_CC + CONV_WIDTH
OFF_Q = OFF_ZC + CONV_WIDTH
OFF_K = OFF_Q + ATTN_WIDTH
OFF_V = OFF_K + KV_WIDTH
OFF_ZA = OFF_V + KV_WIDTH
OFF_GATE = OFF_ZA + ATTN_WIDTH
IN_COLS = OFF_GATE + N_BRANCHES * D_MODEL

LANES = 128
SUBLANES = 8
MXU_WIDTH = 256
TQ = 512
NC = 2 * MXU_WIDTH
HEADS_PER_LANE_BLOCK = LANES // HEAD_DIM
GROUP = N_Q_HEADS // N_KV_HEADS
LANE_BLOCKS_PER_GROUP = GROUP // HEADS_PER_LANE_BLOCK
ATTN_LAG = 2
VMEM_LIMIT_BYTES = 60 * 1024 * 1024

SLOT_TO_HEAD_IN_GROUP = tuple(
    lb * HEADS_PER_LANE_BLOCK + half
    for half in range(HEADS_PER_LANE_BLOCK) for lb in range(LANE_BLOCKS_PER_GROUP))

_BF16 = jnp.bfloat16
_F32 = jnp.float32


def _dot(a, b):
    return jnp.dot(a, b, preferred_element_type=_F32)


def _dot_scores(q, k):
    return lax.dot_general(q, k, (((1,), (1,)), ((), ())), preferred_element_type=_F32)


def _dot_pv(p, v):
    return jnp.dot(p, v, preferred_element_type=_F32)


def _head_rms_norm(x, gain, low_half):
    sq = x * x
    ss_lo = jnp.sum(jnp.where(low_half, sq, 0.0), axis=-1, keepdims=True)
    ss_hi = jnp.sum(jnp.where(low_half, 0.0, sq), axis=-1, keepdims=True)
    r_lo = lax.rsqrt(ss_lo * (1.0 / HEAD_DIM) + EPS)
    r_hi = lax.rsqrt(ss_hi * (1.0 / HEAD_DIM) + EPS)
    return (x * jnp.where(low_half, r_lo, r_hi)) * gain


N_WEIGHTS = 4


def _layer_kernel(layer, cast_next, sinks_ref, x_ref, ng_ref, convw_ref, qg_ref, kg_ref, gb_ref,
                  win_ref, wco_ref, wao_ref, wout_ref, *refs):
    n_cast = N_WEIGHTS if cast_next else 0
    next_f32_refs, refs = refs[:n_cast], refs[n_cast:]
    out_ref, refs = refs[0], refs[1:]
    next_bf16_refs, refs = refs[:n_cast], refs[n_cast:]
    (h_ref, cvcarry_ref, yc_ref, qn_ref,
     kk_ref, vlo_ref, vhi_ref, za_ref, o_ref, gateb_ref) = refs
    gya_ref = out_ref.at[0]
    t = pl.program_id(1)
    tq = x_ref.shape[1]
    n_blocks = tq // BLOCK

    @pl.when(t == 0)
    def _():
        cvcarry_ref[...] = jnp.zeros_like(cvcarry_ref)
        kk_ref[:, 0:BLOCK, :] = jnp.zeros((N_KV_HEADS, BLOCK, LANES), _BF16)
        vlo_ref[:, 0:BLOCK, :] = jnp.zeros((N_KV_HEADS, BLOCK, LANES), _BF16)
        vhi_ref[:, 0:BLOCK, :] = jnp.zeros((N_KV_HEADS, BLOCK, LANES), _BF16)

    x = x_ref[0]
    ms = jnp.mean(x * x, axis=-1, keepdims=True)
    h_ref[...] = ((x * lax.rsqrt(ms + EPS)) * ng_ref[...]).astype(_BF16)

    def project(col, width=NC):
        return _dot(h_ref[...], win_ref[:, col:col + width])

    lane = lax.broadcasted_iota(jnp.int32, (tq, LANES), 1)
    low_half = lane < HEAD_DIM
    scale = HEAD_DIM ** -0.5 * LOG2_E
    groups_per_matmul = NC // (GROUP * HEAD_DIM)
    q_issued, za_issued = set(), set()

    def project_q(n):
        q_issued.add(n)
        q = project(OFF_Q + n * NC)
        for j in range(NC // LANES):
            m = n * (NC // LANES) + j
            qs = _head_rms_norm(q[:, j * LANES:(j + 1) * LANES], qg_ref[...], low_half) * scale
            qn_ref[:, m * LANES:(m + 1) * LANES] = qs.astype(_BF16)

    project_q(0)

    assert OFF_V == OFF_K + KV_WIDTH
    kv = project(OFF_K, 2 * KV_WIDTH)
    cur = slice(BLOCK, BLOCK + tq)
    for mk in range(KV_WIDTH // LANES):
        kp = _head_rms_norm(kv[:, mk * LANES:(mk + 1) * LANES], kg_ref[...], low_half)
        kr = pltpu.roll(kp, HEAD_DIM, axis=1)
        vp = kv[:, KV_WIDTH + mk * LANES:KV_WIDTH + (mk + 1) * LANES]
        vr = pltpu.roll(vp, HEAD_DIM, axis=1)
        g_even = mk * HEADS_PER_LANE_BLOCK
        g_odd = g_even + 1
        kk_ref[g_even, cur, :] = jnp.where(low_half, kp, kr).astype(_BF16)
        kk_ref[g_odd, cur, :] = jnp.where(low_half, kr, kp).astype(_BF16)
        ones_hi = jnp.where(lane == HEAD_DIM, 1.0, 0.0)
        ones_lo = jnp.where(lane == 0, 1.0, 0.0)
        vlo_ref[g_even, cur, :] = jnp.where(low_half, vp, ones_hi).astype(_BF16)
        vhi_ref[g_even, cur, :] = jnp.where(low_half, ones_lo, vr).astype(_BF16)
        vlo_ref[g_odd, cur, :] = jnp.where(low_half, vr, ones_hi).astype(_BF16)
        vhi_ref[g_odd, cur, :] = jnp.where(low_half, ones_lo, vp).astype(_BF16)

    def project_za(n):
        za_issued.add(n)
        za_ref[:, n * NC:(n + 1) * NC] = jax.nn.silu(project(OFF_ZA + n * NC))

    project_za(0)

    row8 = lax.broadcasted_iota(jnp.int32, (SUBLANES, MXU_WIDTH), 0)

    def merge_gate(branch, n):
        col = branch * D_MODEL + n * NC
        return jax.nn.sigmoid(project(OFF_GATE + col) + gb_ref[:, col:col + NC])

    def independent_stages():
        for n in range(1, ATTN_WIDTH // NC):
            project_q(n)
            yield
            project_za(n)
            yield
        gates = [(1, n) for n in range(D_MODEL // NC)] + [(0, n) for n in range(D_MODEL // NC)]
        for c in range(CONV_WIDTH // MXU_WIDTH):
            cols = slice(c * MXU_WIDTH, (c + 1) * MXU_WIDTH)
            cv = project(OFF_CC + c * MXU_WIDTH, MXU_WIDTH) * project(OFF_VC + c * MXU_WIDTH, MXU_WIDTH)
            carry = cvcarry_ref[:, cols]
            cvcarry_ref[:, cols] = cv[tq - SUBLANES:tq, :]

            def delayed(k):
                r = pltpu.roll(cv, k, axis=0)
                head = jnp.where(row8 < k, pltpu.roll(carry, k, axis=0), r[0:SUBLANES, :])
                return jnp.concatenate([head, r[SUBLANES:, :]], axis=0)

            w = convw_ref[:, cols]
            y = delayed(CONV_K - 1) * w[0:1, :]
            for tap in range(1, CONV_K - 1):
                y = y + delayed(CONV_K - 1 - tap) * w[tap:tap + 1, :]
            y = y + cv * w[CONV_K - 1:CONV_K, :]
            yield
            by = project(OFF_BC + c * MXU_WIDTH, MXU_WIDTH) * y
            yc_ref[:, cols] = (by * jax.nn.silu(project(OFF_ZC + c * MXU_WIDTH, MXU_WIDTH))).astype(_BF16)
            yield
            branch, n = gates.pop(0)
            gate_ref = gateb_ref if branch == 1 else gya_ref
            gate_ref[:, n * NC:(n + 1) * NC] = merge_gate(branch, n)
            yield
        assert not gates
        for n in range(D_MODEL // NC):
            cols = slice(n * NC, (n + 1) * NC)
            gya_ref[:, cols] = gya_ref[:, cols] * _dot(yc_ref[...], wco_ref[:, cols])
            yield

    qi = lax.broadcasted_iota(jnp.int32, (BLOCK, 2 * BLOCK), 0)
    kc = lax.broadcasted_iota(jnp.int32, (BLOCK, 2 * BLOCK), 1)
    dist = kc - qi
    in_band = (dist > 0) & (dist <= WINDOW)
    band = jnp.where(in_band, jnp.inf, NEG_INF)
    band_first = jnp.where(in_band & ((kc >= BLOCK) | (t > 0)), jnp.inf, NEG_INF)
    low_half_block = lax.broadcasted_iota(jnp.int32, (BLOCK, LANES), 1) < HEAD_DIM
    low_half_pair = lax.broadcasted_iota(
        jnp.int32, (LANE_BLOCKS_PER_GROUP * BLOCK, LANES), 1) < HEAD_DIM

    def softmax_unnormalised(s, cap, sink):
        s = jnp.minimum(s, cap)
        m = jnp.maximum(jnp.max(s, axis=-1, keepdims=True), sink)
        p = jnp.exp2(s - m)
        return p.astype(_BF16), jnp.exp2(sink - m)

    def scores(b, g):
        assert g // groups_per_matmul in q_issued
        rows = slice(b * BLOCK, (b + 1) * BLOCK)
        win = slice(b * BLOCK, b * BLOCK + 2 * BLOCK)
        blocks = [qn_ref[rows, (g * LANE_BLOCKS_PER_GROUP + i) * LANES:(g * LANE_BLOCKS_PER_GROUP + i + 1) * LANES]
                  for i in range(LANE_BLOCKS_PER_GROUP)]
        zero = jnp.zeros((BLOCK, LANES), _BF16)
        q_stack = jnp.concatenate([jnp.where(low_half_block, blk, zero) for blk in blocks]
                                  + [jnp.where(low_half_block, zero, blk) for blk in blocks], axis=0)
        return _dot_scores(q_stack, kk_ref[g, win, :])

    def attend(b, g, s):
        assert g // groups_per_matmul in za_issued
        rows = slice(b * BLOCK, (b + 1) * BLOCK)
        win = slice(b * BLOCK, b * BLOCK + 2 * BLOCK)
        cap = band_first if b == 0 else band
        ps, dens = [], []
        for slot in range(GROUP):
            head = g * GROUP + SLOT_TO_HEAD_IN_GROUP[slot]
            p, den = softmax_unnormalised(s[slot * BLOCK:(slot + 1) * BLOCK], cap,
                                          sinks_ref[layer, head] * LOG2_E)
            ps.append(p)
            dens.append(den)
        n_low = LANE_BLOCKS_PER_GROUP
        o2 = _dot_pv(jnp.concatenate(ps, axis=0),
                     jnp.concatenate([vlo_ref[g, win, :], vhi_ref[g, win, :]], axis=1))
        o_low, o_high = o2[0:n_low * BLOCK, 0:LANES], o2[n_low * BLOCK:, LANES:]
        o = jnp.where(low_half_pair, o_low, o_high)
        den = jnp.where(low_half_pair,
                        o_low[:, HEAD_DIM:HEAD_DIM + 1] + jnp.concatenate(dens[:n_low], axis=0),
                        o_high[:, 0:1] + jnp.concatenate(dens[n_low:], axis=0))
        o = o / den
        for i in range(LANE_BLOCKS_PER_GROUP):
            lb = slice((g * LANE_BLOCKS_PER_GROUP + i) * LANES, (g * LANE_BLOCKS_PER_GROUP + i + 1) * LANES)
            o_ref[rows, lb] = (o[i * BLOCK:(i + 1) * BLOCK] * za_ref[rows, lb]).astype(_BF16)

    steps = [(b, g) for g in range(N_KV_HEADS) for b in range(n_blocks)]
    independent = independent_stages()
    in_flight = {}
    for i in range(len(steps) + ATTN_LAG):
        if i < len(steps):
            in_flight[i] = scores(*steps[i])
        if i >= ATTN_LAG:
            next(independent, None)
            attend(*steps[i - ATTN_LAG], in_flight.pop(i - ATTN_LAG))
    for _ in independent:
        pass

    last = slice(tq, tq + BLOCK)
    kk_ref[:, 0:BLOCK, :] = kk_ref[:, last, :]
    vlo_ref[:, 0:BLOCK, :] = vlo_ref[:, last, :]
    vhi_ref[:, 0:BLOCK, :] = vhi_ref[:, last, :]

    merged = []
    for n in range(D_MODEL // NC):
        cols = slice(n * NC, (n + 1) * NC)
        y_b = _dot(o_ref[...], wao_ref[:, cols])
        merged.append((gya_ref[:, cols] + gateb_ref[:, cols] * y_b).astype(_BF16))
    y = x_ref[0]
    for n in range(D_MODEL // NC):
        y = y + _dot(merged[n], wout_ref[n * NC:(n + 1) * NC, :])
    out_ref[0] = y

    for src_ref, dst_ref in zip(next_f32_refs, next_bf16_refs):
        dst_ref[...] = src_ref[...].astype(_BF16)


def _layer_resident(layer, shape):
    zeros = (0,) * len(shape)
    return pl.BlockSpec((None,) + shape, lambda b, t: (layer,) + zeros, pipeline_mode=pl.Buffered(1))


def _resident(shape):
    return pl.BlockSpec(shape, lambda b, t: (0,) * len(shape), pipeline_mode=pl.Buffered(1))


def _hybrid_layer(layer, tq, x, sinks, small_params, weights, next_weights_f32):
    batch, seq, d = x.shape
    assert d == D_MODEL and seq % tq == 0 and tq % BLOCK == 0
    assert len(weights) == N_WEIGHTS and weights[0].shape == (D_MODEL, IN_COLS)
    steps_per_seq = seq // tq
    grid = (batch, steps_per_seq)
    kv_rows = tq + BLOCK
    cast_next = next_weights_f32 is not None

    in_specs = [
        pl.BlockSpec(memory_space=pltpu.SMEM),
        pl.BlockSpec((1, tq, D_MODEL), lambda b, t: (b, t, 0)),
        _layer_resident(layer, (1, D_MODEL)),
        _layer_resident(layer, (CONV_K, CONV_WIDTH)),
        _layer_resident(layer, (1, LANES)),
        _layer_resident(layer, (1, LANES)),
        _layer_resident(layer, (1, N_BRANCHES * D_MODEL)),
    ] + [_resident(w.shape) for w in weights]
    out_specs = [pl.BlockSpec((1, tq, D_MODEL), lambda b, t: (b, t, 0))]
    out_shape = [jax.ShapeDtypeStruct(x.shape, x.dtype)]
    operands = [sinks, x, *small_params, *weights]
    if cast_next:
        n_steps = batch * steps_per_seq
        for w in next_weights_f32:
            rows, cols = w.shape[1:]
            assert rows % n_steps == 0
            rows_per_step = rows // n_steps
            in_specs.append(pl.BlockSpec((None, rows_per_step, cols),
                                         lambda b, t: (layer + 1, b * steps_per_seq + t, 0)))
            out_specs.append(pl.BlockSpec((rows_per_step, cols), lambda b, t: (b * steps_per_seq + t, 0)))
            out_shape.append(jax.ShapeDtypeStruct((rows, cols), _BF16))
            operands.append(w)

    outs = pl.pallas_call(
        functools.partial(_layer_kernel, layer, cast_next),
        grid=grid,
        in_specs=in_specs,
        out_specs=out_specs,
        out_shape=out_shape,
        scratch_shapes=[
            pltpu.VMEM((tq, D_MODEL), _BF16),
            pltpu.VMEM((SUBLANES, CONV_WIDTH), _F32),
            pltpu.VMEM((tq, CONV_WIDTH), _BF16),
            pltpu.VMEM((tq, ATTN_WIDTH), _BF16),
            pltpu.VMEM((N_KV_HEADS, kv_rows, LANES), _BF16),
            pltpu.VMEM((N_KV_HEADS, kv_rows, LANES), _BF16),
            pltpu.VMEM((N_KV_HEADS, kv_rows, LANES), _BF16),
            pltpu.VMEM((tq, ATTN_WIDTH), _F32),
            pltpu.VMEM((tq, ATTN_WIDTH), _BF16),
            pltpu.VMEM((tq, D_MODEL), _F32),
        ],
        compiler_params=pltpu.CompilerParams(
            dimension_semantics=("arbitrary", "arbitrary"),
            vmem_limit_bytes=VMEM_LIMIT_BYTES,
        ),
        name=f"hybrid_layer_{layer}",
    )(*operands)
    return outs[0], tuple(outs[1:])


def kernel(x, norm_g, w_in, conv_w, q_norm_g, k_norm_g, sinks, w_conv_out, w_attn_out, gate_b, w_out):
    depth = norm_g.shape[0]
    tq = min(TQ, x.shape[1])
    small_params = (
        norm_g[:, None, :],
        conv_w,
        jnp.tile(q_norm_g, (1, HEADS_PER_LANE_BLOCK))[:, None, :],
        jnp.tile(k_norm_g, (1, HEADS_PER_LANE_BLOCK))[:, None, :],
        gate_b[:, None, :],
    )
    weights_f32 = (w_in, w_conv_out, w_attn_out, w_out)
    weights = tuple(w[0].astype(_BF16) for w in weights_f32)
    for layer in range(depth):
        x, weights = _hybrid_layer(layer, tq, x, sinks, small_params, weights,
                                   weights_f32 if layer + 1 < depth else None)
    return x
```
